```python
import math
import jax
import jax.numpy as jnp
from jax import lax
import numpy as np

D_MODEL = 2048
BATCH = 2
SEQ = 4096
DEPTH = 2
DEC_BATCH = 32
DEC_SEQ = 1
PAST_LEN = 8192
PAGE_SIZE = 128

HEAD_DIM = 128
SB_HEADS = D_MODEL // (2 * HEAD_DIM)
SB_Q_BLOCK = 128
SB_BIAS_INIT = -6.0
GDN_HEADS = D_MODEL // (4 * HEAD_DIM)
GDN_DK = HEAD_DIM
GDN_DV = HEAD_DIM
GDN_CONV = 4
GDN_CHUNK = 64
GLA_HEADS = D_MODEL // (4 * HEAD_DIM)
GLA_DK = HEAD_DIM // 2
GLA_DV = HEAD_DIM
GLA_RANK = 16
GLA_TAU = 16.0
GLA_CHUNK = 64
SB_WIDTH = SB_HEADS * HEAD_DIM
GDN_WIDTH = GDN_HEADS * GDN_DV
GLA_WIDTH = GLA_HEADS * GLA_DV
MIX_WIDTH = SB_WIDTH + GDN_WIDTH + GLA_WIDTH
CONV_CH = 2 * GDN_HEADS * GDN_DK + GDN_WIDTH
IN_SIZES = (SB_WIDTH, SB_WIDTH, SB_WIDTH,
            GDN_HEADS * GDN_DK, GDN_HEADS * GDN_DK, GDN_WIDTH, GDN_WIDTH, GDN_HEADS, GDN_HEADS,
            GLA_HEADS * GLA_DK, GLA_HEADS * GLA_DK, GLA_WIDTH, GLA_RANK, GLA_WIDTH)
IN_COLS = sum(IN_SIZES)
N_EXPERTS = 32
TOP_K = 4
MOE_FF = D_MODEL
MOE_BLOCK = 256
SWIGLU_LIMIT = 7.0
SWIGLU_ALPHA = 1.702
PLE_DIM = 256
DEEPNORM_ALPHA = (2 * DEPTH) ** 0.25
DEEPNORM_BETA = (8 * DEPTH) ** -0.25
LN_EPS = 1e-5
RMS_EPS = 1e-6
F32 = jnp.float32

kernel_name = 'hybrid_stickbreak_gdn_gla_moe_deepnorm_step'


def _split_points(sizes):
    pts, acc = [], 0
    for s in sizes[:-1]:
        acc += s
        pts.append(acc)
    return pts


def layer_norm(x, g, b):
    xf = x.astype(F32)
    mu = jnp.mean(xf, -1, keepdims=True)
    var = jnp.mean(jnp.square(xf - mu), -1, keepdims=True)
    return ((xf - mu) * lax.rsqrt(var + LN_EPS) * g.astype(F32) + b.astype(F32)).astype(x.dtype)


def l2_normalize(t):
    tf = t.astype(F32)
    return tf * lax.rsqrt(jnp.sum(tf * tf, -1, keepdims=True) + RMS_EPS)


def gated_rms_norm(o, z, w):
    of = o.astype(F32)
    of = of * lax.rsqrt(jnp.mean(of * of, -1, keepdims=True) + RMS_EPS) * w.astype(F32)
    return (of * jax.nn.silu(z.astype(F32))).astype(z.dtype)


def pad_time(t, pad):
    if pad == 0:
        return t
    widths = [(0, 0)] * t.ndim
    widths[1] = (0, pad)
    return jnp.pad(t, widths)


def to_chunks(t, n, c):
    b, h = t.shape[0], t.shape[2]
    t = t.reshape((b, n, c, h) + t.shape[3:])
    return t.transpose((1, 0, 3, 2) + tuple(range(4, t.ndim)))


def from_chunks(t, length):
    n, b, h, c, d = t.shape
    return t.transpose(1, 0, 3, 2, 4).reshape(b, n * c, h, d)[:, :length]


def gather_pages(pool, page_table):
    g = pool[page_table]
    return g.reshape((g.shape[0], g.shape[1] * g.shape[2]) + pool.shape[2:])


def stick_breaking_attention(q, k, v, q_start, bias):
    b, lq, h, dh = q.shape
    lk = k.shape[1]
    qb = min(SB_Q_BLOCK, lq)
    n_blk = -(-lq // qb)
    q = pad_time(q, n_blk * qb - lq)
    q_blocks = q.reshape(b, n_blk, qb, h, dh).transpose(1, 0, 2, 3, 4)
    q_pos = (q_start + jnp.arange(n_blk * qb)).reshape(n_blk, qb)
    k_pos = jnp.arange(lk)
    scale = dh ** -0.5
    bias_f = bias.astype(F32)[None, :, None, None]

    def one_block(args):
        qblk, qp = args
        z = jnp.einsum('bqhd,bkhd->bhqk', qblk, k).astype(F32) * scale + bias_f
        mask = k_pos[None, :] < qp[:, None]
        log_beta = jax.nn.log_sigmoid(z)
        log_keep = jnp.where(mask, jax.nn.log_sigmoid(-z), 0.0)
        after = lax.cumsum(log_keep, axis=3, reverse=True) - log_keep
        w = jnp.where(mask, jnp.exp(log_beta + after), 0.0)
        return jnp.einsum('bhqk,bkhd->bqhd', w.astype(v.dtype), v)

    o = lax.map(one_block, (q_blocks, q_pos))
    return o.transpose(1, 0, 2, 3, 4).reshape(b, n_blk * qb, h, dh)[:, :lq]


def causal_depthwise_conv(x_cat, w):
    return lax.conv_general_dilated(x_cat, w[:, None, :].astype(x_cat.dtype), window_strides=(1,),
                                    padding='VALID', dimension_numbers=('NWC', 'WIO', 'NWC'),
                                    feature_group_count=x_cat.shape[-1])


def gated_delta_rule_chunked(q, k, v, g, beta, s0):
    b, length, h, dk = q.shape
    dv = v.shape[-1]
    c = min(GDN_CHUNK, length)
    n = -(-length // c)
    pad = n * c - length
    qc = to_chunks(pad_time(q.astype(F32) * dk ** -0.5, pad), n, c)
    kc = to_chunks(pad_time(k.astype(F32), pad), n, c)
    vc = to_chunks(pad_time(v.astype(F32), pad), n, c)
    gc = jnp.cumsum(to_chunks(pad_time(g.astype(F32), pad), n, c), axis=-1)
    bc = to_chunks(pad_time(beta.astype(F32), pad), n, c)
    tri = jnp.tril(jnp.ones((c, c), bool))
    strict = jnp.tril(jnp.ones((c, c), bool), -1)
    decay = jnp.exp(jnp.where(tri, gc[..., :, None] - gc[..., None, :], -jnp.inf))
    kk = jnp.einsum('nbhid,nbhjd->nbhij', kc * bc[..., None], kc) * decay
    a_mat = jnp.eye(c, dtype=F32) + jnp.where(strict, kk, 0.0)
    rhs = jnp.concatenate([vc * bc[..., None], kc * (bc * jnp.exp(gc))[..., None]], axis=-1)
    sol = lax.linalg.triangular_solve(a_mat, rhs, left_side=True, lower=True, unit_diagonal=True)
    u, w = sol[..., :dv], sol[..., dv:]
    intra = jnp.where(tri, jnp.einsum('nbhid,nbhjd->nbhij', qc, kc) * decay, 0.0)

    def step(s, xs):
        q_i, k_i, u_i, w_i, g_i, a_i = xs
        v_new = u_i - jnp.einsum('bhcd,bhde->bhce', w_i, s)
        o = (jnp.einsum('bhcd,bhde->bhce', q_i * jnp.exp(g_i)[..., None], s)
             + jnp.einsum('bhij,bhje->bhie', a_i, v_new))
        g_last = g_i[..., -1]
        s = (s * jnp.exp(g_last)[..., None, None]
             + jnp.einsum('bhcd,bhce->bhde', k_i * jnp.exp(g_last[..., None] - g_i)[..., None], v_new))
        return s, o

    s, o = lax.scan(step, s0.astype(F32), (qc, kc, u, w, gc, intra))
    return from_chunks(o, length), s


def gla_chunked(q, k, v, log_a, s0):
    b, length, h, dk = q.shape
    c = min(GLA_CHUNK, length)
    n = -(-length // c)
    pad = n * c - length
    qc = to_chunks(pad_time(q.astype(F32) * dk ** -0.5, pad), n, c)
    kc = to_chunks(pad_time(k.astype(F32), pad), n, c)
    vc = to_chunks(pad_time(v.astype(F32), pad), n, c)
    bc = jnp.cumsum(to_chunks(pad_time(log_a.astype(F32), pad), n, c), axis=3)
    causal = jnp.tril(jnp.ones((c, c), bool))[:, :, None]

    def step(s, xs):
        q_i, k_i, v_i, b_i = xs
        rel = jnp.where(causal, b_i[:, :, :, None, :] - b_i[:, :, None, :, :], -jnp.inf)
        att = jnp.einsum('bhid,bhjd,bhijd->bhij', q_i, k_i, jnp.exp(rel))
        o = (jnp.einsum('bhid,bhde->bhie', q_i * jnp.exp(b_i), s)
             + jnp.einsum('bhij,bhje->bhie', att, v_i))
        b_last = b_i[:, :, -1]
        s = (jnp.exp(b_last)[..., None] * s
             + jnp.einsum('bhjd,bhje->bhde', k_i * jnp.exp(b_last[:, :, None] - b_i), v_i))
        return s, o

    s, o = lax.scan(step, s0.astype(F32), (qc, kc, vc, bc))
    return from_chunks(o, length), s


def moe_ffn(x, router_w, router_b, w_gu, b_gu, w_dn, b_dn):
    b, length, d = x.shape
    t = b * length
    xt = x.reshape(t, d)
    logits = (xt @ router_w).astype(F32) + router_b.astype(F32)
    top_val, top_idx = lax.top_k(logits, TOP_K)
    gates = jax.nn.softmax(top_val, axis=-1)
    n_assign = t * TOP_K
    flat_e = top_idx.reshape(n_assign)
    flat_tok = jnp.arange(n_assign, dtype=jnp.int32) // TOP_K
    flat_gate = gates.reshape(n_assign)
    order = jnp.argsort(flat_e)
    se, stok, sgate = flat_e[order], flat_tok[order], flat_gate[order]
    blk = min(MOE_BLOCK, max(8, n_assign // N_EXPERTS))
    counts = jnp.bincount(flat_e, length=N_EXPERTS)
    padded = (counts + blk - 1) // blk * blk
    pad_end = jnp.cumsum(padded)
    pad_start = pad_end - padded
    start = jnp.cumsum(counts) - counts
    dest = pad_start[se] + (jnp.arange(n_assign) - start[se])
    n_blocks = -(-(n_assign + N_EXPERTS * (blk - 1)) // blk)
    n_rows = n_blocks * blk
    row_tok = jnp.zeros((n_rows,), jnp.int32).at[dest].set(stok)
    row_gate = jnp.zeros((n_rows,), F32).at[dest].set(sgate)
    blk_expert = jnp.minimum(jnp.searchsorted(pad_end, jnp.arange(n_blocks) * blk, side='right'),
                             N_EXPERTS - 1)
    x_rows = xt[row_tok].reshape(n_blocks, blk, d)

    def expert_block(args):
        xb, e = args
        hgu = xb @ w_gu[e] + b_gu[e]
        gate = jnp.minimum(hgu[:, :MOE_FF], SWIGLU_LIMIT)
        up = jnp.clip(hgu[:, MOE_FF:], -SWIGLU_LIMIT, SWIGLU_LIMIT)
        return ((up + 1.0) * (gate * jax.nn.sigmoid(SWIGLU_ALPHA * gate))) @ w_dn[e] + b_dn[e]

    y_rows = lax.map(expert_block, (x_rows, blk_expert)).reshape(n_rows, d)
    y = jnp.zeros((t, d), F32).at[row_tok].add(row_gate[:, None] * y_rows.astype(F32))
    return y.reshape(b, length, d).astype(x.dtype)


def trunk_layer(x, p, past_k, past_v, gdn_s0, conv_buf, gla_s0,
                w_in, sb_bias, conv_w, a_log, dt_bias, gdn_nw, gla_wg, gla_bg, gla_nw, w_out,
                ln_g, ln_b, router_w, router_b, w_gu, b_gu, w_dn, b_dn, ple_wp, ple_wg):
    b, length, _ = x.shape
    proj = jnp.einsum('bld,dc->blc', x, w_in)
    (sb_q, sb_k, sb_v, g_q, g_k, g_v, g_z, g_b, g_a,
     l_q, l_k, l_v, l_r, l_g) = jnp.split(proj, _split_points(IN_SIZES), axis=-1)

    def heads(t, nh):
        return t.reshape(b, length, nh, -1)

    k_new = heads(sb_k, SB_HEADS)
    v_new = heads(sb_v, SB_HEADS)
    keys = jnp.concatenate([past_k.astype(k_new.dtype), k_new], axis=1)
    vals = jnp.concatenate([past_v.astype(v_new.dtype), v_new], axis=1)
    o_sb = stick_breaking_attention(heads(sb_q, SB_HEADS), keys, vals, past_k.shape[1], sb_bias)

    conv_in = jnp.concatenate([conv_buf.astype(proj.dtype), jnp.concatenate([g_q, g_k, g_v], -1)], axis=1)
    new_conv = conv_in[:, -(GDN_CONV - 1):]
    conv_out = jax.nn.silu(causal_depthwise_conv(conv_in, conv_w))
    c_q, c_k, c_v = jnp.split(conv_out, [GDN_HEADS * GDN_DK, 2 * GDN_HEADS * GDN_DK], axis=-1)
    beta = jax.nn.sigmoid(g_b.astype(F32))
    log_decay = -jnp.exp(a_log.astype(F32)) * jax.nn.softplus(g_a.astype(F32) + dt_bias.astype(F32))
    o_gdn, gdn_s = gated_delta_rule_chunked(l2_normalize(heads(c_q, GDN_HEADS)),
                                            l2_normalize(heads(c_k, GDN_HEADS)),
                                            heads(c_v, GDN_HEADS), log_decay, beta, gdn_s0)
    o_gdn = gated_rms_norm(o_gdn, heads(g_z, GDN_HEADS), gdn_nw)

    log_a = jax.nn.log_sigmoid((l_r @ gla_wg + gla_bg).astype(F32)) / GLA_TAU
    o_gla, gla_s = gla_chunked(heads(l_q, GLA_HEADS), heads(l_k, GLA_HEADS),
                               heads(l_v, GLA_HEADS), heads(log_a, GLA_HEADS), gla_s0)
    o_gla = gated_rms_norm(o_gla, heads(l_g, GLA_HEADS), gla_nw)

    mixed = jnp.concatenate([o_sb.reshape(b, length, SB_WIDTH).astype(proj.dtype),
                             o_gdn.reshape(b, length, GDN_WIDTH),
                             o_gla.reshape(b, length, GLA_WIDTH)], axis=-1)
    x = layer_norm(DEEPNORM_ALPHA * x + mixed @ w_out, ln_g[0], ln_b[0])
    x = layer_norm(DEEPNORM_ALPHA * x + moe_ffn(x, router_w, router_b, w_gu, b_gu, w_dn, b_dn),
                   ln_g[1], ln_b[1])
    ple = (p @ ple_wp) * jax.nn.sigmoid(x @ ple_wg)
    x = layer_norm(DEEPNORM_ALPHA * x + ple, ln_g[2], ln_b[2])
    return x, (k_new, v_new, gdn_s, new_conv, gla_s)


def setup_inputs(seed: int = 0) -> dict:
    key = jax.random.key(seed)
    k = jax.random.split(key, 40)
    n_pages = PAST_LEN // PAGE_SIZE
    n_phys = (DEC_BATCH * n_pages * 5) // 4

    def nrm(i, shape, s):
        return jax.random.normal(k[i], shape, F32) * s

    page_table = jax.random.permutation(k[0], n_phys)[:DEC_BATCH * n_pages].reshape(
        DEC_BATCH, n_pages).astype(jnp.int32)
    dt = jnp.exp(jax.random.uniform(k[1], (DEPTH, GDN_HEADS), F32, math.log(1e-3), math.log(1e-1)))
    return {
        'x_prompt': nrm(2, (BATCH, SEQ, D_MODEL), 1.0),
        'x_sample': nrm(3, (DEC_BATCH, DEC_SEQ, D_MODEL), 1.0),
        'cache_k': nrm(4, (DEPTH, n_phys, PAGE_SIZE, SB_HEADS, HEAD_DIM), 1.0),
        'cache_v': nrm(5, (DEPTH, n_phys, PAGE_SIZE, SB_HEADS, HEAD_DIM), 1.0),
        'state_gdn': nrm(6, (DEPTH, DEC_BATCH, GDN_HEADS, GDN_DK, GDN_DV), 0.5),
        'state_conv': nrm(7, (DEPTH, DEC_BATCH, GDN_CONV - 1, CONV_CH), 1.0),
        'state_gla': nrm(8, (DEPTH, DEC_BATCH, GLA_HEADS, GLA_DK, GLA_DV), 1.0),
        'page_table': page_table,
        'p_prompt': nrm(9, (DEPTH, BATCH, SEQ, PLE_DIM), 1.0),
        'p_sample': nrm(10, (DEPTH, DEC_BATCH, DEC_SEQ, PLE_DIM), 1.0),
        'w_in': nrm(11, (DEPTH, D_MODEL, IN_COLS), D_MODEL ** -0.5),
        'sb_logit_bias': SB_BIAS_INIT + nrm(29, (DEPTH, SB_HEADS), 0.1),
        'gdn_conv_w': nrm(12, (DEPTH, GDN_CONV, CONV_CH), GDN_CONV ** -0.5),
        'gdn_a_log': jnp.log(jax.random.uniform(k[13], (DEPTH, GDN_HEADS), F32, 1.0, 16.0)),
        'gdn_dt_bias': dt + jnp.log(-jnp.expm1(-dt)),
        'gdn_norm_w': 1.0 + nrm(14, (DEPTH, GDN_DV), 0.02),
        'gla_w_gate': nrm(15, (DEPTH, GLA_RANK, GLA_HEADS * GLA_DK), GLA_RANK ** -0.5),
        'gla_b_gate': nrm(16, (DEPTH, GLA_HEADS * GLA_DK), 0.1),
        'gla_norm_w': 1.0 + nrm(17, (DEPTH, GLA_DV), 0.02),
        'w_out': nrm(18, (DEPTH, MIX_WIDTH, D_MODEL), MIX_WIDTH ** -0.5 * DEEPNORM_BETA),
        'ln_g': 1.0 + nrm(19, (DEPTH, 3, D_MODEL), 0.02),
        'ln_b': nrm(20, (DEPTH, 3, D_MODEL), 0.02),
        'router_w': nrm(21, (DEPTH, D_MODEL, N_EXPERTS), D_MODEL ** -0.5),
        'router_b': nrm(22, (DEPTH, N_EXPERTS), 0.01),
        'moe_w_gu': nrm(23, (DEPTH, N_EXPERTS, D_MODEL, 2 * MOE_FF), D_MODEL ** -0.5),
        'moe_b_gu': nrm(24, (DEPTH, N_EXPERTS, 2 * MOE_FF), 0.02),
        'moe_w_dn': nrm(25, (DEPTH, N_EXPERTS, MOE_FF, D_MODEL), MOE_FF ** -0.5 * DEEPNORM_BETA),
        'moe_b_dn': nrm(26, (DEPTH, N_EXPERTS, D_MODEL), 0.02),
        'ple_w_proj': nrm(27, (DEPTH, PLE_DIM, D_MODEL), PLE_DIM ** -0.5 * DEEPNORM_BETA),
        'ple_w_gate': nrm(28, (DEPTH, D_MODEL, D_MODEL), D_MODEL ** -0.5),
    }


def reference(x_prompt, x_sample, cache_k, cache_v, state_gdn, state_conv, state_gla, page_table,
              p_prompt, p_sample, w_in, sb_logit_bias, gdn_conv_w, gdn_a_log, gdn_dt_bias, gdn_norm_w,
              gla_w_gate, gla_b_gate, gla_norm_w, w_out, ln_g, ln_b, router_w, router_b,
              moe_w_gu, moe_b_gu, moe_w_dn, moe_b_dn, ple_w_proj, ple_w_gate):
    b = x_prompt.shape[0]
    h_p, h_s = x_prompt, x_sample
    new_p = ([], [], [], [], [])
    new_s = ([], [], [], [], [])
    for i in range(DEPTH):
        lw = (w_in[i], sb_logit_bias[i], gdn_conv_w[i], gdn_a_log[i], gdn_dt_bias[i], gdn_norm_w[i],
              gla_w_gate[i], gla_b_gate[i], gla_norm_w[i], w_out[i], ln_g[i], ln_b[i],
              router_w[i], router_b[i], moe_w_gu[i], moe_b_gu[i], moe_w_dn[i], moe_b_dn[i],
              ple_w_proj[i], ple_w_gate[i])
        empty_kv = jnp.zeros((b, 0, SB_HEADS, HEAD_DIM), x_prompt.dtype)
        h_p, st_p = trunk_layer(h_p, p_prompt[i], empty_kv, empty_kv,
                                jnp.zeros((b, GDN_HEADS, GDN_DK, GDN_DV), F32),
                                jnp.zeros((b, GDN_CONV - 1, CONV_CH), x_prompt.dtype),
                                jnp.zeros((b, GLA_HEADS, GLA_DK, GLA_DV), F32), *lw)
        past_k = gather_pages(cache_k[i], page_table)
        past_v = gather_pages(cache_v[i], page_table)
        h_s, st_s = trunk_layer(h_s, p_sample[i], past_k, past_v, state_gdn[i], state_conv[i],
                                state_gla[i], *lw)
        for lst, a in zip(new_p, st_p):
            lst.append(a)
        for lst, a in zip(new_s, st_s):
            lst.append(a)
    k_prompt = jnp.stack(new_p[0])
    v_prompt = jnp.stack(new_p[1])
    gdn_prompt = jnp.stack(new_p[2])
    conv_prompt = jnp.stack(new_p[3])
    gla_prompt = jnp.stack(new_p[4])
    k_sample = jnp.stack(new_s[0])
    v_sample = jnp.stack(new_s[1])
    gdn_sample = jnp.stack(new_s[2])
    conv_sample = jnp.stack(new_s[3])
    gla_sample = jnp.stack(new_s[4])
    return (h_p, h_s, k_prompt, v_prompt, gdn_prompt, conv_prompt, gla_prompt,
            k_sample, v_sample, gdn_sample, conv_sample, gla_sample)
```

```python
import functools

import jax
import jax.numpy as jnp
from jax import lax
from jax.experimental import pallas as pl
from jax.experimental.pallas import tpu as pltpu

F32 = jnp.float32
BF16 = jnp.bfloat16

D_MODEL = 2048
HEAD_DIM = 128
SB_HEADS = 8
GDN_HEADS = 4
GDN_DK = 128
GDN_DV = 128
GDN_CONV = 4
GLA_HEADS = 4
GLA_DK = 64
GLA_DV = 128
GLA_RANK = 16
GLA_TAU = 16.0
CHUNK = 64
GLA_SUB = 16
SB_WIDTH = SB_HEADS * HEAD_DIM
GDN_WIDTH = GDN_HEADS * GDN_DV
GLA_WIDTH = GLA_HEADS * GLA_DV
CONV_CH = 2 * GDN_HEADS * GDN_DK + GDN_WIDTH
N_EXPERTS = 32
TOP_K = 4
MOE_FF = D_MODEL
SWIGLU_LIMIT = 7.0
SWIGLU_ALPHA = 1.702
LN_EPS = 1e-5
RMS_EPS = 1e-6

COL_SBQ = 0
COL_SBK = 1024
COL_SBV = 2048
COL_CONV = 3072
COL_GZ = 4608
COL_LQKV = 5120
COL_LG = 6144
COL_MISC = 6656
MISC_BETA = 0
MISC_A = 4
MISC_R = 8
N_PROJ = 6912
PROJ_TN = 768

LANES = 128
VMEM_LIMIT = 56 * 1024 * 1024

MOE_BLK = 256
MOE_TN = 1024

SB_TQ = 512
SB_SUB = 128
DEC_PAGES = 8
PAGE = 128


def _cparams(sem):
    return pltpu.CompilerParams(dimension_semantics=sem, vmem_limit_bytes=VMEM_LIMIT)


def _softplus(x):
    return jnp.maximum(x, 0.0) + jnp.log1p(jnp.exp(-jnp.abs(x)))


def _log_sigmoid(x):
    return jnp.minimum(x, 0.0) - jnp.log1p(jnp.exp(-jnp.abs(x)))


def _sigmoid(x):
    return 1.0 / (1.0 + jnp.exp(-x))


def _silu(x):
    return x * _sigmoid(x)


def _iota(shape, dim):
    return lax.broadcasted_iota(jnp.int32, shape, dim)


def _contract(a, b, dims):
    precision = lax.Precision.HIGHEST if (a.dtype == F32 and b.dtype == F32) else None
    return lax.dot_general(a, b, (dims, ((), ())), precision=precision, preferred_element_type=F32)


def _dot(a, b):
    return _contract(a, b, ((1,), (0,)))


def _dot_nt(a, b):
    return _contract(a, b, ((1,), (1,)))


def _dot_tn(a, b):
    return _contract(a, b, ((0,), (0,)))


def _rev_excl_cumsum_lanes(x, upper):
    hi = x.astype(BF16)
    lo = (x - hi.astype(F32)).astype(BF16)
    return _dot(hi, upper) + _dot(lo, upper)


def _layer_norm(h, g, b):
    mu = jnp.mean(h, axis=-1, keepdims=True)
    d = h - mu
    var = jnp.mean(d * d, axis=-1, keepdims=True)
    return d * lax.rsqrt(var + LN_EPS) * g + b


def _gated_rms(o, z, w):
    of = o * lax.rsqrt(jnp.mean(o * o, axis=-1, keepdims=True) + RMS_EPS) * w
    return of * _silu(z)


def _proj_kernel(x_ref, w_ref, o_ref):
    o_ref[...] = _dot(x_ref[...].astype(BF16), w_ref[...])


def _project(x, w, tm):
    m, k = x.shape
    n = w.shape[1]
    return pl.pallas_call(
        _proj_kernel,
        grid=(m // tm, n // PROJ_TN),
        in_specs=[pl.BlockSpec((tm, k), lambda i, j: (i, 0)),
                  pl.BlockSpec((k, PROJ_TN), lambda i, j: (0, j))],
        out_specs=pl.BlockSpec((tm, PROJ_TN), lambda i, j: (i, j)),
        out_shape=jax.ShapeDtypeStruct((m, n), F32),
        compiler_params=_cparams(("parallel", "arbitrary")),
        name="in_proj",
    )(x, w)


def _sb_prompt_kernel(qt_ref, kt_ref, q_ref, k_ref, v_ref, bias_ref, o_ref, acc_ref, carry_ref):
    p = pl.program_id(2)
    qi = qt_ref[p]
    kj = kt_ref[p]
    tq = q_ref.shape[0]
    nsub = k_ref.shape[0] // SB_SUB
    scale = HEAD_DIM ** -0.5
    bias = bias_ref[...]
    upper = (_iota((SB_SUB, SB_SUB), 0) > _iota((SB_SUB, SB_SUB), 1)).astype(BF16)

    @pl.when(kj == qi)
    def _():
        acc_ref[...] = jnp.zeros_like(acc_ref)
        carry_ref[...] = jnp.zeros_like(carry_ref)

    def sweep(diagonal):
        for c in reversed(range(nsub)):
            r0 = c * SB_SUB if diagonal else 0
            rows = tq - r0
            q = q_ref[r0:tq, :].astype(BF16)
            k = k_ref[c * SB_SUB:(c + 1) * SB_SUB, :].astype(BF16)
            v = v_ref[c * SB_SUB:(c + 1) * SB_SUB, :].astype(BF16)
            z = _dot_nt(q, k) * scale + bias
            sp = _softplus(z)
            log_keep = -sp
            if diagonal:
                visible = (_iota((rows, SB_SUB), 1) + c * SB_SUB) < (_iota((rows, SB_SUB), 0) + r0)
                log_keep = jnp.where(visible, log_keep, 0.0)
            carry = carry_ref[r0:tq, :]
            after = _rev_excl_cumsum_lanes(log_keep, upper) + carry
            w = jnp.exp(z - sp + after)
            if diagonal:
                w = jnp.where(visible, w, 0.0)
            acc_ref[r0:tq, :] += _dot(w.astype(BF16), v)
            carry_ref[r0:tq, :] = carry + jnp.sum(log_keep, axis=1, keepdims=True)

    @pl.when(kj == qi)
    def _():
        sweep(True)

    @pl.when(kj != qi)
    def _():
        sweep(False)

    @pl.when(kj == 0)
    def _():
        o_ref[...] = acc_ref[...].astype(o_ref.dtype)


def _sb_prompt(proj3, bias_lanes):
    b, seq, _ = proj3.shape
    nq = seq // SB_TQ
    q_of, k_of = [], []
    for qi in range(nq):
        for kj in range(qi, -1, -1):
            q_of.append(qi)
            k_of.append(kj)
    q_tab = jnp.asarray(q_of, jnp.int32)
    k_tab = jnp.asarray(k_of, jnp.int32)
    kb = COL_SBK // HEAD_DIM
    vb = COL_SBV // HEAD_DIM
    grid_spec = pltpu.PrefetchScalarGridSpec(
        num_scalar_prefetch=2,
        grid=(b, SB_HEADS, len(q_of)),
        in_specs=[
            pl.BlockSpec((None, SB_TQ, HEAD_DIM), lambda bi, h, p, qt, kt: (bi, qt[p], h)),
            pl.BlockSpec((None, SB_TQ, HEAD_DIM), lambda bi, h, p, qt, kt: (bi, kt[p], kb + h)),
            pl.BlockSpec((None, SB_TQ, HEAD_DIM), lambda bi, h, p, qt, kt: (bi, kt[p], vb + h)),
            pl.BlockSpec((None, 1, LANES), lambda bi, h, p, qt, kt: (h, 0, 0)),
        ],
        out_specs=pl.BlockSpec((None, SB_TQ, HEAD_DIM), lambda bi, h, p, qt, kt: (bi, qt[p], h)),
        scratch_shapes=[pltpu.VMEM((SB_TQ, HEAD_DIM), F32), pltpu.VMEM((SB_TQ, LANES), F32)],
    )
    return pl.pallas_call(
        _sb_prompt_kernel,
        grid_spec=grid_spec,
        out_shape=jax.ShapeDtypeStruct((b, seq, SB_WIDTH), BF16),
        compiler_params=_cparams(("parallel", "parallel", "arbitrary")),
        name="sb_prompt",
    )(q_tab, k_tab, proj3, proj3, proj3, bias_lanes)


def _sb_decode_kernel(pt_ref, q_ref, bias_ref, *refs):
    k_refs = refs[:DEC_PAGES]
    v_refs = refs[DEC_PAGES:2 * DEC_PAGES]
    o_ref, acc_ref, carry_ref = refs[2 * DEC_PAGES:]
    g = pl.program_id(1)
    scale = HEAD_DIM ** -0.5
    rows = PAGE * SB_HEADS
    upper = (_iota((LANES, LANES), 0) > _iota((LANES, LANES), 1)).astype(BF16)
    own_head = jnp.bitwise_and(_iota((SB_HEADS, rows), 1), SB_HEADS - 1) == _iota((SB_HEADS, rows), 0)

    @pl.when(g == 0)
    def _():
        acc_ref[...] = jnp.zeros_like(acc_ref)
        carry_ref[...] = jnp.zeros_like(carry_ref)

    q = q_ref[...].astype(BF16)
    bias = bias_ref[...]
    for p in reversed(range(DEC_PAGES)):
        k = k_refs[p][...].astype(BF16)
        v = v_refs[p][...].astype(BF16)
        z = _dot_nt(q, k) * scale + bias
        sp = _softplus(z)
        log_keep = jnp.where(own_head, -sp, 0.0)
        run = carry_ref[...]
        after = [None] * (rows // LANES)
        for c in reversed(range(rows // LANES)):
            lk = log_keep[:, c * LANES:(c + 1) * LANES]
            after[c] = _rev_excl_cumsum_lanes(lk, upper) + run
            run = run + jnp.sum(lk, axis=1, keepdims=True)
        carry_ref[...] = run
        w = jnp.where(own_head, jnp.exp(z - sp + jnp.concatenate(after, axis=1)), 0.0)
        acc_ref[...] += _dot(w.astype(BF16), v)

    @pl.when(g == pl.num_programs(1) - 1)
    def _():
        o_ref[...] = acc_ref[...]


def _sb_decode(q_heads, bias_rows, cache_k3, cache_v3, page_table, layer, n_phys):
    nb = q_heads.shape[0]
    rows = PAGE * SB_HEADS
    n_pages = page_table.shape[1]
    n_groups = n_pages // DEC_PAGES
    base = layer * n_phys

    def page_map(p):
        def index_map(bi, g, pt):
            return (base + pt[bi, (n_groups - 1 - g) * DEC_PAGES + p], 0, 0)
        return index_map

    page_specs = [pl.BlockSpec((None, rows, HEAD_DIM), page_map(p)) for p in range(DEC_PAGES)]
    grid_spec = pltpu.PrefetchScalarGridSpec(
        num_scalar_prefetch=1,
        grid=(nb, n_groups),
        in_specs=[pl.BlockSpec((None, SB_HEADS, HEAD_DIM), lambda bi, g, pt: (bi, 0, 0)),
                  pl.BlockSpec((SB_HEADS, rows), lambda bi, g, pt: (0, 0))] + page_specs + page_specs,
        out_specs=pl.BlockSpec((None, SB_HEADS, HEAD_DIM), lambda bi, g, pt: (bi, 0, 0)),
        scratch_shapes=[pltpu.VMEM((SB_HEADS, HEAD_DIM), F32), pltpu.VMEM((SB_HEADS, LANES), F32)],
    )
    return pl.pallas_call(
        _sb_decode_kernel,
        grid_spec=grid_spec,
        out_shape=jax.ShapeDtypeStruct((nb, SB_HEADS, HEAD_DIM), F32),
        compiler_params=_cparams(("parallel", "arbitrary")),
        name="sb_decode",
    )(page_table, q_heads, bias_rows, *([cache_k3] * DEC_PAGES), *([cache_v3] * DEC_PAGES))


def _unit_lower_inverse(n_strict):
    eye = (_iota((CHUNK, CHUNK), 0) == _iota((CHUNK, CHUNK), 1)).astype(F32)
    m = -n_strict
    t = eye + m
    power = 2
    while power < CHUNK:
        m = _dot(m, m)
        t = t + _dot(t, m)
        power *= 2
    return t


def _gdn_prompt_kernel(x_ref, z_ref, misc_ref, convw_ref, par_ref, nw_ref, o_ref, s_ref,
                       xbuf, ybuf, gbuf, bbuf):
    l = pl.program_id(1)
    lb = x_ref.shape[0]
    halo = 8

    @pl.when(l == 0)
    def _():
        s_ref[...] = jnp.zeros_like(s_ref)
        xbuf[0:halo, :] = jnp.zeros((halo, CONV_CH), F32)

    @pl.when(l != 0)
    def _():
        xbuf[0:halo, :] = xbuf[lb:lb + halo, :]

    xbuf[halo:halo + lb, :] = x_ref[...]
    y = jnp.zeros((lb, CONV_CH), F32)
    for i in range(GDN_CONV):
        off = halo - (GDN_CONV - 1) + i
        y = y + xbuf[off:off + lb, :] * convw_ref[i:i + 1, :]
    ybuf[...] = _silu(y)

    misc = misc_ref[...]
    neg_rate = -jnp.exp(par_ref[0:1, :])
    gbuf[...] = neg_rate * _softplus(misc + par_ref[1:2, :])
    bbuf[...] = _sigmoid(misc)

    row = _iota((CHUNK, CHUNK), 0)
    col = _iota((CHUNK, CHUNK), 1)
    lower = row >= col
    strict = row > col
    eye = (row == col).astype(F32)
    lower_f = lower.astype(F32)
    nw = nw_ref[...]

    def chunk_body(c, carry):
        r = pl.multiple_of(c * CHUNK, CHUNK)
        gcum = _dot(lower_f, gbuf[pl.ds(r, CHUNK), :])
        beta_all = bbuf[pl.ds(r, CHUNK), :]
        for h in range(GDN_HEADS):
            q = ybuf[pl.ds(r, CHUNK), h * GDN_DK:(h + 1) * GDN_DK]
            k = ybuf[pl.ds(r, CHUNK), (GDN_HEADS + h) * GDN_DK:(GDN_HEADS + h + 1) * GDN_DK]
            v = ybuf[pl.ds(r, CHUNK), 2 * GDN_HEADS * GDN_DK + h * GDN_DV:
                     2 * GDN_HEADS * GDN_DK + (h + 1) * GDN_DV]
            q = q * lax.rsqrt(jnp.sum(q * q, axis=-1, keepdims=True) + RMS_EPS) * (GDN_DK ** -0.5)
            k = k * lax.rsqrt(jnp.sum(k * k, axis=-1, keepdims=True) + RMS_EPS)
            beta = beta_all[:, MISC_BETA + h:MISC_BETA + h + 1]
            gcol = gcum[:, MISC_A + h:MISC_A + h + 1]
            grow = jnp.sum(eye * gcol, axis=0, keepdims=True)
            decay = jnp.where(lower, jnp.exp(jnp.minimum(gcol - grow, 0.0)), 0.0)
            kb = k * beta
            kk = _dot_nt(kb, k) * decay
            t_inv = _unit_lower_inverse(jnp.where(strict, kk, 0.0))
            e_g = jnp.exp(gcol)
            u = _dot(t_inv, v * beta)
            w = _dot(t_inv, kb * e_g)
            intra = _dot_nt(q, k) * decay
            s = s_ref[h]
            v_new = u - _dot(w, s)
            o = _dot(q * e_g, s) + _dot(intra, v_new)
            g_last = gcol[CHUNK - 1:CHUNK, :]
            s_ref[h] = s * jnp.exp(g_last) + _dot_tn(k * jnp.exp(g_last - gcol), v_new)
            zg = z_ref[pl.ds(r, CHUNK), h * GDN_DV:(h + 1) * GDN_DV]
            o_ref[pl.ds(r, CHUNK), h * GDN_DV:(h + 1) * GDN_DV] = _gated_rms(o, zg, nw).astype(o_ref.dtype)
        return carry

    lax.fori_loop(0, lb // CHUNK, chunk_body, 0)


GDN_LB = 1024


def _gdn_prompt(proj3, conv_w, par, nw):
    b, seq, _ = proj3.shape
    lb = GDN_LB
    return pl.pallas_call(
        _gdn_prompt_kernel,
        grid=(b, seq // lb),
        in_specs=[
            pl.BlockSpec((None, lb, CONV_CH), lambda bi, l: (bi, l, COL_CONV // CONV_CH)),
            pl.BlockSpec((None, lb, GDN_WIDTH), lambda bi, l: (bi, l, COL_GZ // GDN_WIDTH)),
            pl.BlockSpec((None, lb, LANES), lambda bi, l: (bi, l, COL_MISC // LANES)),
            pl.BlockSpec((GDN_CONV, CONV_CH), lambda bi, l: (0, 0)),
            pl.BlockSpec((8, LANES), lambda bi, l: (0, 0)),
            pl.BlockSpec((1, GDN_DV), lambda bi, l: (0, 0)),
        ],
        out_specs=[
            pl.BlockSpec((None, lb, GDN_WIDTH), lambda bi, l: (bi, l, 0)),
            pl.BlockSpec((None, GDN_HEADS, GDN_DK, GDN_DV), lambda bi, l: (bi, 0, 0, 0)),
        ],
        out_shape=[jax.ShapeDtypeStruct((b, seq, GDN_WIDTH), BF16),
                   jax.ShapeDtypeStruct((b, GDN_HEADS, GDN_DK, GDN_DV), F32)],
        scratch_shapes=[pltpu.VMEM((lb + 8, CONV_CH), F32), pltpu.VMEM((lb, CONV_CH), F32),
                        pltpu.VMEM((lb, LANES), F32), pltpu.VMEM((lb, LANES), F32)],
        compiler_params=_cparams(("parallel", "arbitrary")),
        name="gdn_prompt",
    )(proj3, proj3, proj3, conv_w, par, nw)


def _gla_prompt_kernel(x_ref, gate_ref, misc_ref, wg_ref, bg_ref, nw_ref, o_ref, s_ref, abuf):
    l = pl.program_id(1)
    lb = x_ref.shape[0]
    kw = GLA_HEADS * GLA_DK

    @pl.when(l == 0)
    def _():
        s_ref[...] = jnp.zeros_like(s_ref)

    abuf[...] = _log_sigmoid(_dot(misc_ref[...], wg_ref[...]) + bg_ref[...]) * (1.0 / GLA_TAU)

    row = _iota((CHUNK, CHUNK), 0)
    col = _iota((CHUNK, CHUNK), 1)
    lower = row >= col
    lower_f = lower.astype(F32)
    eye_k = (_iota((GLA_DK, GLA_DK), 0) == _iota((GLA_DK, GLA_DK), 1)).astype(F32)
    key_row = _iota((CHUNK, GLA_DK), 0)
    nw = nw_ref[...]

    def chunk_body(c, carry):
        r = pl.multiple_of(c * CHUNK, CHUNK)
        bcum = _dot(lower_f, abuf[pl.ds(r, CHUNK), :])
        for h in range(GLA_HEADS):
            q = x_ref[pl.ds(r, CHUNK), h * GLA_DK:(h + 1) * GLA_DK] * (GLA_DK ** -0.5)
            k = x_ref[pl.ds(r, CHUNK), kw + h * GLA_DK:kw + (h + 1) * GLA_DK]
            v = x_ref[pl.ds(r, CHUNK), 2 * kw + h * GLA_DV:2 * kw + (h + 1) * GLA_DV]
            bh = bcum[:, h * GLA_DK:(h + 1) * GLA_DK]
            pieces = []
            for i in range(CHUNK // GLA_SUB):
                lo, hi = i * GLA_SUB, (i + 1) * GLA_SUB
                ref = bh[lo:lo + 1, :]
                qn = q[lo:hi, :] * jnp.exp(bh[lo:hi, :] - ref)
                kn = k * jnp.exp(jnp.where(key_row < hi, ref - bh, 0.0))
                pieces.append(_dot_nt(qn, kn))
            att = jnp.where(lower, jnp.concatenate(pieces, axis=0), 0.0)
            s = s_ref[h]
            o = _dot(q * jnp.exp(bh), s) + _dot(att, v)
            b_last = bh[CHUNK - 1:CHUNK, :]
            s_ref[h] = _dot(eye_k * jnp.exp(b_last), s) + _dot_tn(k * jnp.exp(b_last - bh), v)
            zg = gate_ref[pl.ds(r, CHUNK), h * GLA_DV:(h + 1) * GLA_DV]
            o_ref[pl.ds(r, CHUNK), h * GLA_DV:(h + 1) * GLA_DV] = _gated_rms(o, zg, nw).astype(o_ref.dtype)
        return carry

    lax.fori_loop(0, lb // CHUNK, chunk_body, 0)


GLA_LB = 1024


def _gla_prompt(proj3, wg_pad, bg, nw):
    b, seq, _ = proj3.shape
    lb = GLA_LB
    xw = 2 * GLA_HEADS * GLA_DK + GLA_WIDTH
    return pl.pallas_call(
        _gla_prompt_kernel,
        grid=(b, seq // lb),
        in_specs=[
            pl.BlockSpec((None, lb, xw), lambda bi, l: (bi, l, COL_LQKV // xw)),
            pl.BlockSpec((None, lb, GLA_WIDTH), lambda bi, l: (bi, l, COL_LG // GLA_WIDTH)),
            pl.BlockSpec((None, lb, LANES), lambda bi, l: (bi, l, COL_MISC // LANES)),
            pl.BlockSpec((LANES, GLA_HEADS * GLA_DK), lambda bi, l: (0, 0)),
            pl.BlockSpec((1, GLA_HEADS * GLA_DK), lambda bi, l: (0, 0)),
            pl.BlockSpec((1, GLA_DV), lambda bi, l: (0, 0)),
        ],
        out_specs=[
            pl.BlockSpec((None, lb, GLA_WIDTH), lambda bi, l: (bi, l, 0)),
            pl.BlockSpec((None, GLA_HEADS, GLA_DK, GLA_DV), lambda bi, l: (bi, 0, 0, 0)),
        ],
        out_shape=[jax.ShapeDtypeStruct((b, seq, GLA_WIDTH), BF16),
                   jax.ShapeDtypeStruct((b, GLA_HEADS, GLA_DK, GLA_DV), F32)],
        scratch_shapes=[pltpu.VMEM((lb, GLA_HEADS * GLA_DK), F32)],
        compiler_params=_cparams(("parallel", "arbitrary")),
        name="gla_prompt",
    )(proj3, proj3, proj3, wg_pad, bg, nw)


def _to_column(row, n):
    eye = (_iota((n, n), 0) == _iota((n, n), 1)).astype(F32)
    return jnp.sum(eye * row, axis=1, keepdims=True)


def _recurrent_step_kernel(x_ref, sg_ref, sc_ref, sl_ref, convw_ref, par_ref, gnw_ref,
                           wg_ref, bg_ref, lnw_ref, og_ref, ol_ref, sg_out, sc_out, sl_out):
    xrow = x_ref[...]
    new = xrow[:, COL_CONV:COL_CONV + CONV_CH]
    old = sc_ref[...]
    sc_out[...] = jnp.concatenate([old[1:GDN_CONV - 1, :], new], axis=0)
    y = new * convw_ref[GDN_CONV - 1:GDN_CONV, :]
    for i in range(GDN_CONV - 1):
        y = y + old[i:i + 1, :] * convw_ref[i:i + 1, :]
    y = _silu(y)

    misc = xrow[:, COL_MISC:COL_MISC + LANES]
    g_all = -jnp.exp(par_ref[0:1, :]) * _softplus(misc + par_ref[1:2, :])
    beta_all = _sigmoid(misc)
    gnw = gnw_ref[...]
    for h in range(GDN_HEADS):
        q = y[:, h * GDN_DK:(h + 1) * GDN_DK]
        k = y[:, (GDN_HEADS + h) * GDN_DK:(GDN_HEADS + h + 1) * GDN_DK]
        v = y[:, 2 * GDN_HEADS * GDN_DK + h * GDN_DV:2 * GDN_HEADS * GDN_DK + (h + 1) * GDN_DV]
        q = q * lax.rsqrt(jnp.sum(q * q, axis=-1, keepdims=True) + RMS_EPS) * (GDN_DK ** -0.5)
        k = k * lax.rsqrt(jnp.sum(k * k, axis=-1, keepdims=True) + RMS_EPS)
        beta = beta_all[:, MISC_BETA + h:MISC_BETA + h + 1]
        e_g = jnp.exp(g_all[:, MISC_A + h:MISC_A + h + 1])
        s = sg_ref[h]
        kcol = _to_column(k, GDN_DK)
        qcol = _to_column(q, GDN_DK)
        v_new = beta * (v - e_g * jnp.sum(s * kcol, axis=0, keepdims=True))
        qk = jnp.sum(q * k, axis=-1, keepdims=True)
        o = e_g * jnp.sum(s * qcol, axis=0, keepdims=True) + qk * v_new
        sg_out[h] = s * e_g + kcol * v_new
        zg = xrow[:, COL_GZ + h * GDN_DV:COL_GZ + (h + 1) * GDN_DV]
        og_ref[:, h * GDN_DV:(h + 1) * GDN_DV] = _gated_rms(o, zg, gnw).astype(og_ref.dtype)

    kw = GLA_HEADS * GLA_DK
    log_a = _log_sigmoid(_dot(misc, wg_ref[...]) + bg_ref[...]) * (1.0 / GLA_TAU)
    lnw = lnw_ref[...]
    for h in range(GLA_HEADS):
        q = xrow[:, COL_LQKV + h * GLA_DK:COL_LQKV + (h + 1) * GLA_DK] * (GLA_DK ** -0.5)
        k = xrow[:, COL_LQKV + kw + h * GLA_DK:COL_LQKV + kw + (h + 1) * GLA_DK]
        v = xrow[:, COL_LQKV + 2 * kw + h * GLA_DV:COL_LQKV + 2 * kw + (h + 1) * GLA_DV]
        a = jnp.exp(log_a[:, h * GLA_DK:(h + 1) * GLA_DK])
        s = sl_ref[h]
        qk = jnp.sum(q * k, axis=-1, keepdims=True)
        o = jnp.sum(s * _to_column(q * a, GLA_DK), axis=0, keepdims=True) + qk * v
        sl_out[h] = _to_column(a, GLA_DK) * s + _to_column(k, GLA_DK) * v
        zg = xrow[:, COL_LG + h * GLA_DV:COL_LG + (h + 1) * GLA_DV]
        ol_ref[:, h * GLA_DV:(h + 1) * GLA_DV] = _gated_rms(o, zg, lnw).astype(ol_ref.dtype)


def _recurrent_step(proj_s3, state_gdn, state_conv, state_gla, layer, conv_w, par, gnw, wg_pad, bg, lnw):
    nb = proj_s3.shape[0]
    const2 = lambda bi: (0, 0)
    return pl.pallas_call(
        _recurrent_step_kernel,
        grid=(nb,),
        in_specs=[
            pl.BlockSpec((None, 1, N_PROJ), lambda bi: (bi, 0, 0)),
            pl.BlockSpec((None, None, GDN_HEADS, GDN_DK, GDN_DV), lambda bi: (layer, bi, 0, 0, 0)),
            pl.BlockSpec((None, None, GDN_CONV - 1, CONV_CH), lambda bi: (layer, bi, 0, 0)),
            pl.BlockSpec((None, None, GLA_HEADS, GLA_DK, GLA_DV), lambda bi: (layer, bi, 0, 0, 0)),
            pl.BlockSpec((GDN_CONV, CONV_CH), const2),
            pl.BlockSpec((8, LANES), const2),
            pl.BlockSpec((1, GDN_DV), const2),
            pl.BlockSpec((LANES, GLA_HEADS * GLA_DK), const2),
            pl.BlockSpec((1, GLA_HEADS * GLA_DK), const2),
            pl.BlockSpec((1, GLA_DV), const2),
        ],
        out_specs=[
            pl.BlockSpec((None, 1, GDN_WIDTH), lambda bi: (bi, 0, 0)),
            pl.BlockSpec((None, 1, GLA_WIDTH), lambda bi: (bi, 0, 0)),
            pl.BlockSpec((None, GDN_HEADS, GDN_DK, GDN_DV), lambda bi: (bi, 0, 0, 0)),
            pl.BlockSpec((None, GDN_CONV - 1, CONV_CH), lambda bi: (bi, 0, 0)),
            pl.BlockSpec((None, GLA_HEADS, GLA_DK, GLA_DV), lambda bi: (bi, 0, 0, 0)),
        ],
        out_shape=[
            jax.ShapeDtypeStruct((nb, 1, GDN_WIDTH), F32),
            jax.ShapeDtypeStruct((nb, 1, GLA_WIDTH), F32),
            jax.ShapeDtypeStruct((nb, GDN_HEADS, GDN_DK, GDN_DV), F32),
            jax.ShapeDtypeStruct((nb, GDN_CONV - 1, CONV_CH), F32),
            jax.ShapeDtypeStruct((nb, GLA_HEADS, GLA_DK, GLA_DV), F32),
        ],
        compiler_params=_cparams(("parallel",)),
        name="recurrent_step",
    )(proj_s3, state_gdn, state_conv, state_gla, conv_w, par, gnw, wg_pad, bg, lnw)


def _mix_out_kernel(x_ref, osb_ref, ogdn_ref, ogla_ref, w_ref, g_ref, b_ref, rw_ref, rb_ref,
                    x1_ref, x1h_ref, logit_ref, *, alpha):
    mixed = (_dot(osb_ref[...].astype(BF16), w_ref[0:SB_WIDTH, :])
             + _dot(ogdn_ref[...].astype(BF16), w_ref[SB_WIDTH:SB_WIDTH + GDN_WIDTH, :])
             + _dot(ogla_ref[...].astype(BF16), w_ref[SB_WIDTH + GDN_WIDTH:, :]))
    x1 = _layer_norm(alpha * x_ref[...] + mixed, g_ref[...], b_ref[...])
    x1_ref[...] = x1
    x1h_ref[...] = x1.astype(BF16)
    logit_ref[...] = _dot(x1, rw_ref[...]) + rb_ref[...]


def _mix_out(x, o_sb, o_gdn, o_gla, w_out, ln_g, ln_b, router_w, router_b, alpha, tm):
    m, d = x.shape
    row = lambda i: (i, 0)
    const = lambda i: (0, 0)
    return pl.pallas_call(
        functools.partial(_mix_out_kernel, alpha=alpha),
        grid=(m // tm,),
        in_specs=[
            pl.BlockSpec((tm, d), row),
            pl.BlockSpec((tm, SB_WIDTH), row),
            pl.BlockSpec((tm, GDN_WIDTH), row),
            pl.BlockSpec((tm, GLA_WIDTH), row),
            pl.BlockSpec((d, d), const),
            pl.BlockSpec((1, d), const),
            pl.BlockSpec((1, d), const),
            pl.BlockSpec((d, LANES), const),
            pl.BlockSpec((1, LANES), const),
        ],
        out_specs=[pl.BlockSpec((tm, d), row), pl.BlockSpec((tm, d), row), pl.BlockSpec((tm, LANES), row)],
        out_shape=[jax.ShapeDtypeStruct((m, d), F32), jax.ShapeDtypeStruct((m, d), BF16),
                   jax.ShapeDtypeStruct((m, LANES), F32)],
        compiler_params=_cparams(("parallel",)),
        name="mix_out_ln_router",
    )(x, o_sb, o_gdn, o_gla, w_out, ln_g, ln_b, router_w, router_b)


def _expert_changed(be_ref, i):
    prev = be_ref[jnp.maximum(i - 1, 0)]
    return jnp.logical_or(i == 0, be_ref[i] != prev)


def _moe_up_kernel(be_ref, x_ref, wg_ref, wu_ref, bg_ref, bu_ref, h_ref, wg_bf, wu_bf):
    i = pl.program_id(1)

    @pl.when(_expert_changed(be_ref, i))
    def _():
        wg_bf[...] = wg_ref[...].astype(BF16)
        wu_bf[...] = wu_ref[...].astype(BF16)

    x = x_ref[...]
    gate = jnp.minimum(_dot(x, wg_bf[...]) + bg_ref[...], SWIGLU_LIMIT)
    up = jnp.clip(_dot(x, wu_bf[...]) + bu_ref[...], -SWIGLU_LIMIT, SWIGLU_LIMIT)
    h_ref[...] = ((up + 1.0) * (gate * _sigmoid(SWIGLU_ALPHA * gate))).astype(h_ref.dtype)


def _moe_down_kernel(be_ref, h_ref, w_ref, b_ref, gate_ref, y_ref, w_bf):
    i = pl.program_id(1)

    @pl.when(_expert_changed(be_ref, i))
    def _():
        w_bf[...] = w_ref[...].astype(BF16)

    y_ref[...] = (_dot(h_ref[...], w_bf[...]) + b_ref[...]) * gate_ref[...]


def _moe_experts(x_rows, row_gate, blk_expert, w_gu, b_gu4, w_dn, b_dn4, layer):
    n_rows, d = x_rows.shape
    n_blocks = n_rows // MOE_BLK
    nj = MOE_FF // MOE_TN
    up_spec = pltpu.PrefetchScalarGridSpec(
        num_scalar_prefetch=1,
        grid=(nj, n_blocks),
        in_specs=[
            pl.BlockSpec((MOE_BLK, d), lambda j, i, be: (i, 0)),
            pl.BlockSpec((None, None, d, MOE_TN), lambda j, i, be: (layer, be[i], 0, j)),
            pl.BlockSpec((None, None, d, MOE_TN), lambda j, i, be: (layer, be[i], 0, nj + j)),
            pl.BlockSpec((None, None, 1, MOE_TN), lambda j, i, be: (layer, be[i], 0, j)),
            pl.BlockSpec((None, None, 1, MOE_TN), lambda j, i, be: (layer, be[i], 0, nj + j)),
        ],
        out_specs=pl.BlockSpec((MOE_BLK, MOE_TN), lambda j, i, be: (i, j)),
        scratch_shapes=[pltpu.VMEM((d, MOE_TN), BF16), pltpu.VMEM((d, MOE_TN), BF16)],
    )
    hidden = pl.pallas_call(
        _moe_up_kernel,
        grid_spec=up_spec,
        out_shape=jax.ShapeDtypeStruct((n_rows, MOE_FF), BF16),
        compiler_params=_cparams(("arbitrary", "arbitrary")),
        name="moe_up",
    )(blk_expert, x_rows, w_gu, w_gu, b_gu4, b_gu4)

    nj_dn = d // MOE_TN
    down_spec = pltpu.PrefetchScalarGridSpec(
        num_scalar_prefetch=1,
        grid=(nj_dn, n_blocks),
        in_specs=[
            pl.BlockSpec((MOE_BLK, MOE_FF), lambda j, i, be: (i, 0)),
            pl.BlockSpec((None, None, MOE_FF, MOE_TN), lambda j, i, be: (layer, be[i], 0, j)),
            pl.BlockSpec((None, None, 1, MOE_TN), lambda j, i, be: (layer, be[i], 0, j)),
            pl.BlockSpec((MOE_BLK, 1), lambda j, i, be: (i, 0)),
        ],
        out_specs=pl.BlockSpec((MOE_BLK, MOE_TN), lambda j, i, be: (i, j)),
        scratch_shapes=[pltpu.VMEM((MOE_FF, MOE_TN), BF16)],
    )
    return pl.pallas_call(
        _moe_down_kernel,
        grid_spec=down_spec,
        out_shape=jax.ShapeDtypeStruct((n_rows, d), F32),
        compiler_params=_cparams(("arbitrary", "arbitrary")),
        name="moe_down",
    )(blk_expert, hidden, w_dn, b_dn4, row_gate)


def _route(logits):
    t = logits.shape[0]
    top_val, top_idx = lax.top_k(logits, TOP_K)
    gates = jax.nn.softmax(top_val, axis=-1)
    n_assign = t * TOP_K
    flat_e = top_idx.reshape(n_assign).astype(jnp.int32)
    flat_gate = gates.reshape(n_assign)
    order = jnp.argsort(flat_e).astype(jnp.int32)
    rank = jnp.argsort(order).astype(jnp.int32)
    counts = jnp.bincount(flat_e, length=N_EXPERTS).astype(jnp.int32)
    padded = (counts + MOE_BLK - 1) // MOE_BLK * MOE_BLK
    pad_end = jnp.cumsum(padded)
    pad_start = pad_end - padded
    start = jnp.cumsum(counts) - counts
    n_blocks = -(-(n_assign + N_EXPERTS * (MOE_BLK - 1)) // MOE_BLK)
    blk_expert = jnp.minimum(
        jnp.searchsorted(pad_end, jnp.arange(n_blocks) * MOE_BLK, side='right'), N_EXPERTS - 1
    ).astype(jnp.int32)
    pos = (jnp.arange(n_blocks, dtype=jnp.int32) * MOE_BLK - pad_start[blk_expert])[:, None] \
        + jnp.arange(MOE_BLK, dtype=jnp.int32)[None, :]
    live = pos < counts[blk_expert][:, None]
    src = jnp.where(live, start[blk_expert][:, None] + pos, 0).reshape(-1)
    assign = order[src]
    live = live.reshape(-1)
    row_tok = jnp.where(live, assign // TOP_K, 0)
    row_gate = jnp.where(live, flat_gate[assign], 0.0)
    dest = ((pad_start - start)[flat_e] + rank).reshape(t, TOP_K)
    return row_tok, row_gate, blk_expert, dest


def _ffn_out_kernel(x1_ref, moe_ref, p_ref, wp_ref, wg_ref, g_ref, b_ref, x3_ref, *, alpha):
    g = g_ref[...]
    b = b_ref[...]
    x2 = _layer_norm(alpha * x1_ref[...] + moe_ref[...], g[0:1, :], b[0:1, :])
    ple = _dot(p_ref[...].astype(BF16), wp_ref[...]) * _sigmoid(_dot(x2.astype(BF16), wg_ref[...]))
    x3_ref[...] = _layer_norm(alpha * x2 + ple, g[1:2, :], b[1:2, :])


def _ffn_out(x1, moe, p, wp, wg, ln_g2, ln_b2, alpha, tm):
    m, d = x1.shape
    pd = p.shape[1]
    row = lambda i: (i, 0)
    const = lambda i: (0, 0)
    return pl.pallas_call(
        functools.partial(_ffn_out_kernel, alpha=alpha),
        grid=(m // tm,),
        in_specs=[
            pl.BlockSpec((tm, d), row),
            pl.BlockSpec((tm, d), row),
            pl.BlockSpec((tm, pd), row),
            pl.BlockSpec((pd, d), const),
            pl.BlockSpec((d, d), const),
            pl.BlockSpec((2, d), const),
            pl.BlockSpec((2, d), const),
        ],
        out_specs=pl.BlockSpec((tm, d), row),
        out_shape=jax.ShapeDtypeStruct((m, d), F32),
        compiler_params=_cparams(("parallel",)),
        name="ffn_out_ple_ln",
    )(x1, moe, p, wp, wg, ln_g2, ln_b2)


def _permute_w_in(w):
    c = 5 * SB_WIDTH
    g_b = w[:, c:c + GDN_HEADS]
    g_a = w[:, c + GDN_HEADS:c + 2 * GDN_HEADS]
    c += 2 * GDN_HEADS
    kw = GLA_HEADS * GLA_DK
    l_q = w[:, c:c + kw]
    l_k = w[:, c + kw:c + 2 * kw]
    l_v = w[:, c + 2 * kw:c + 2 * kw + GLA_WIDTH]
    c += 2 * kw + GLA_WIDTH
    l_r = w[:, c:c + GLA_RANK]
    l_g = w[:, c + GLA_RANK:c + GLA_RANK + GLA_WIDTH]
    used = COL_MISC + 2 * GDN_HEADS + GLA_RANK
    pad = jnp.zeros((w.shape[0], N_PROJ - used), w.dtype)
    return jnp.concatenate([w[:, :5 * SB_WIDTH], l_q, l_k, l_v, l_g, g_b, g_a, l_r, pad], axis=1).astype(BF16)


def kernel(x_prompt, x_sample, cache_k, cache_v, state_gdn, state_conv, state_gla, page_table, p_prompt, p_sample, w_in, sb_logit_bias, gdn_conv_w, gdn_a_log, gdn_dt_bias, gdn_norm_w, gla_w_gate, gla_b_gate, gla_norm_w, w_out, ln_g, ln_b, router_w, router_b, moe_w_gu, moe_b_gu, moe_w_dn, moe_b_dn, ple_w_proj, ple_w_gate):
    depth = w_in.shape[0]
    b, seq, d = x_prompt.shape
    nb = x_sample.shape[0]
    t_p = b * seq
    n_phys = cache_k.shape[1]
    alpha = (2 * depth) ** 0.25

    cache_k3 = cache_k.reshape(depth * n_phys, PAGE * SB_HEADS, HEAD_DIM)
    cache_v3 = cache_v.reshape(depth * n_phys, PAGE * SB_HEADS, HEAD_DIM)
    b_gu4 = moe_b_gu.reshape(depth, N_EXPERTS, 1, 2 * MOE_FF)
    b_dn4 = moe_b_dn.reshape(depth, N_EXPERTS, 1, d)

    h_p = x_prompt.reshape(t_p, d)
    h_s = x_sample.reshape(nb, d)
    outs_p = ([], [], [], [], [])
    outs_s = ([], [], [], [], [])
    for i in range(depth):
        w_in_i = _permute_w_in(w_in[i])
        w_out_i = w_out[i].astype(BF16)
        wp_i = ple_w_proj[i].astype(BF16)
        wgate_i = ple_w_gate[i].astype(BF16)
        bias_lanes = jnp.broadcast_to(sb_logit_bias[i][:, None, None], (SB_HEADS, 1, LANES))
        bias_rows = jnp.broadcast_to(sb_logit_bias[i][:, None], (SB_HEADS, PAGE * SB_HEADS))
        par = jnp.zeros((8, LANES), F32)
        par = par.at[0, MISC_A:MISC_A + GDN_HEADS].set(gdn_a_log[i])
        par = par.at[1, MISC_A:MISC_A + GDN_HEADS].set(gdn_dt_bias[i])
        gnw = gdn_norm_w[i].reshape(1, GDN_DV)
        lnw = gla_norm_w[i].reshape(1, GLA_DV)
        wg_pad = jnp.zeros((LANES, GLA_HEADS * GLA_DK), F32).at[MISC_R:MISC_R + GLA_RANK].set(gla_w_gate[i])
        bg = gla_b_gate[i].reshape(1, GLA_HEADS * GLA_DK)
        rw = jnp.zeros((d, LANES), F32).at[:, :N_EXPERTS].set(router_w[i])
        rb = jnp.zeros((1, LANES), F32).at[0, :N_EXPERTS].set(router_b[i])
        g0, b0 = ln_g[i, 0:1], ln_b[i, 0:1]

        proj_p = _project(h_p, w_in_i, 1024)
        proj_p3 = proj_p.reshape(b, seq, N_PROJ)
        o_sb = _sb_prompt(proj_p3, bias_lanes)
        o_gdn, gdn_state_p = _gdn_prompt(proj_p3, gdn_conv_w[i], par, gnw)
        o_gla, gla_state_p = _gla_prompt(proj_p3, wg_pad, bg, lnw)
        x1_p, x1h_p, logit_p = _mix_out(h_p, o_sb.reshape(t_p, SB_WIDTH), o_gdn.reshape(t_p, GDN_WIDTH),
                                        o_gla.reshape(t_p, GLA_WIDTH), w_out_i, g0, b0, rw, rb, alpha, 256)
        outs_p[0].append(proj_p3[:, :, COL_SBK:COL_SBK + SB_WIDTH].reshape(b, seq, SB_HEADS, HEAD_DIM))
        outs_p[1].append(proj_p3[:, :, COL_SBV:COL_SBV + SB_WIDTH].reshape(b, seq, SB_HEADS, HEAD_DIM))
        outs_p[2].append(gdn_state_p)
        outs_p[3].append(proj_p3[:, seq - (GDN_CONV - 1):, COL_CONV:COL_CONV + CONV_CH])
        outs_p[4].append(gla_state_p)

        proj_s = _project(h_s, w_in_i, nb)
        proj_s3 = proj_s.reshape(nb, 1, N_PROJ)
        q_heads_s = proj_s[:, COL_SBQ:COL_SBQ + SB_WIDTH].reshape(nb, SB_HEADS, HEAD_DIM)
        o_sb_s = _sb_decode(q_heads_s, bias_rows, cache_k3, cache_v3, page_table, i, n_phys)
        o_gdn_s, o_gla_s, gdn_state_s, conv_state_s, gla_state_s = _recurrent_step(
            proj_s3, state_gdn, state_conv, state_gla, i, gdn_conv_w[i], par, gnw, wg_pad, bg, lnw)
        x1_s, x1h_s, logit_s = _mix_out(h_s, o_sb_s.reshape(nb, SB_WIDTH), o_gdn_s.reshape(nb, GDN_WIDTH),
                                        o_gla_s.reshape(nb, GLA_WIDTH), w_out_i, g0, b0, rw, rb, alpha, nb)
        outs_s[0].append(proj_s[:, COL_SBK:COL_SBK + SB_WIDTH].reshape(nb, 1, SB_HEADS, HEAD_DIM))
        outs_s[1].append(proj_s[:, COL_SBV:COL_SBV + SB_WIDTH].reshape(nb, 1, SB_HEADS, HEAD_DIM))
        outs_s[2].append(gdn_state_s)
        outs_s[3].append(conv_state_s)
        outs_s[4].append(gla_state_s)

        logits = jnp.concatenate([logit_p[:, :N_EXPERTS], logit_s[:, :N_EXPERTS]], axis=0)
        row_tok, row_gate, blk_e, dest = _route(logits)
        x_rows = jnp.concatenate([x1h_p, x1h_s], axis=0)[row_tok]
        y_rows = _moe_experts(x_rows, row_gate[:, None], blk_e, moe_w_gu, b_gu4, moe_w_dn, b_dn4, i)
        moe_all = jnp.sum(y_rows[dest], axis=1)
        moe_p, moe_s = moe_all[:t_p], moe_all[t_p:]

        g12, b12 = ln_g[i, 1:3], ln_b[i, 1:3]
        h_p = _ffn_out(x1_p, moe_p, p_prompt[i].reshape(t_p, -1), wp_i, wgate_i, g12, b12, alpha, 256)
        h_s = _ffn_out(x1_s, moe_s, p_sample[i].reshape(nb, -1), wp_i, wgate_i, g12, b12, alpha, nb)

    y_prompt = h_p.reshape(b, seq, d)
    y_sample = h_s.reshape(nb, 1, d)
    return (y_prompt, y_sample,
            jnp.stack(outs_p[0]), jnp.stack(outs_p[1]), jnp.stack(outs_p[2]), jnp.stack(outs_p[3]),
            jnp.stack(outs_p[4]),
            jnp.stack(outs_s[0]), jnp.stack(outs_s[1]), jnp.stack(outs_s[2]), jnp.stack(outs_s[3]),
            jnp.stack(outs_s[4]))
```

```python
import functools

import jax
import jax.numpy as jnp
from jax import lax
from jax.experimental import pallas as pl
from jax.experimental.pallas import tpu as pltpu

F32 = jnp.float32
BF16 = jnp.bfloat16

D_MODEL = 2048
HEAD_DIM = 128
SB_HEADS = 8
GDN_HEADS = 4
GDN_DK = 128
GDN_DV = 128
GDN_CONV = 4
GLA_HEADS = 4
GLA_DK = 64
GLA_DV = 128
GLA_RANK = 16
GLA_TAU = 16.0
CHUNK = 64
GLA_SUB = 16
SB_WIDTH = SB_HEADS * HEAD_DIM
GDN_WIDTH = GDN_HEADS * GDN_DV
GLA_WIDTH = GLA_HEADS * GLA_DV
CONV_CH = 2 * GDN_HEADS * GDN_DK + GDN_WIDTH
N_EXPERTS = 32
TOP_K = 4
MOE_FF = D_MODEL
SWIGLU_LIMIT = 7.0
SWIGLU_ALPHA = 1.702
LN_EPS = 1e-5
RMS_EPS = 1e-6

COL_SBQ = 0
COL_SBK = 1024
COL_SBV = 2048
COL_CONV = 3072
COL_GZ = 4608
COL_LQKV = 5120
COL_LG = 6144
COL_MISC = 6656
MISC_BETA = 0
MISC_A = 4
MISC_R = 8
N_PROJ = 6912
PROJ_TN = 768

LANES = 128
VMEM_LIMIT = 56 * 1024 * 1024

MOE_BLK = 256
MOE_TN = 1024

SB_TQ = 512
SB_SUB = 128
DEC_PAGES = 8
PAGE = 128


def _cparams(sem):
    return pltpu.CompilerParams(dimension_semantics=sem, vmem_limit_bytes=VMEM_LIMIT)


def _softplus(x):
    return jnp.maximum(x, 0.0) + jnp.log(1.0 + jnp.exp(-jnp.abs(x)))


def _log_sigmoid(x):
    return jnp.minimum(x, 0.0) - jnp.log(1.0 + jnp.exp(-jnp.abs(x)))


def _sigmoid(x):
    return 1.0 / (1.0 + jnp.exp(-x))


def _silu(x):
    return x * _sigmoid(x)


def _iota(shape, dim):
    return lax.broadcasted_iota(jnp.int32, shape, dim)


def _contract(a, b, dims):
    precision = lax.Precision.HIGHEST if (a.dtype == F32 and b.dtype == F32) else None
    return lax.dot_general(a, b, (dims, ((), ())), precision=precision, preferred_element_type=F32)


def _dot(a, b):
    return _contract(a, b, ((1,), (0,)))


def _dot_nt(a, b):
    return _contract(a, b, ((1,), (1,)))


def _dot_tn(a, b):
    return _contract(a, b, ((0,), (0,)))


def _rev_excl_cumsum_lanes(x, upper):
    hi = x.astype(BF16)
    lo = (x - hi.astype(F32)).astype(BF16)
    return _dot(hi, upper) + _dot(lo, upper)


def _layer_norm(h, g, b):
    mu = jnp.mean(h, axis=-1, keepdims=True)
    d = h - mu
    var = jnp.mean(d * d, axis=-1, keepdims=True)
    return d * lax.rsqrt(var + LN_EPS) * g + b


def _gated_rms(o, z, w):
    of = o * lax.rsqrt(jnp.mean(o * o, axis=-1, keepdims=True) + RMS_EPS) * w
    return of * _silu(z)


def _proj_kernel(x_ref, w_ref, o_ref):
    o_ref[...] = _dot(x_ref[...].astype(BF16), w_ref[...])


def _project(x, w, tm):
    m, k = x.shape
    n = w.shape[1]
    return pl.pallas_call(
        _proj_kernel,
        grid=(m // tm, n // PROJ_TN),
        in_specs=[pl.BlockSpec((tm, k), lambda i, j: (i, 0)),
                  pl.BlockSpec((k, PROJ_TN), lambda i, j: (0, j))],
        out_specs=pl.BlockSpec((tm, PROJ_TN), lambda i, j: (i, j)),
        out_shape=jax.ShapeDtypeStruct((m, n), F32),
        compiler_params=_cparams(("parallel", "arbitrary")),
        name="in_proj",
    )(x, w)


def _sb_prompt_kernel(qt_ref, kt_ref, q_ref, k_ref, v_ref, bias_ref, o_ref, acc_ref, carry_ref):
    p = pl.program_id(2)
    qi = qt_ref[p]
    kj = kt_ref[p]
    tq = q_ref.shape[0]
    nsub = k_ref.shape[0] // SB_SUB
    scale = HEAD_DIM ** -0.5
    bias = bias_ref[...]
    upper = (_iota((SB_SUB, SB_SUB), 0) > _iota((SB_SUB, SB_SUB), 1)).astype(BF16)

    @pl.when(kj == qi)
    def _():
        acc_ref[...] = jnp.zeros_like(acc_ref)
        carry_ref[...] = jnp.zeros_like(carry_ref)

    def sweep(diagonal):
        for c in reversed(range(nsub)):
            r0 = c * SB_SUB if diagonal else 0
            rows = tq - r0
            q = q_ref[r0:tq, :].astype(BF16)
            k = k_ref[c * SB_SUB:(c + 1) * SB_SUB, :].astype(BF16)
            v = v_ref[c * SB_SUB:(c + 1) * SB_SUB, :].astype(BF16)
            z = _dot_nt(q, k) * scale + bias
            sp = _softplus(z)
            log_keep = -sp
            if diagonal:
                visible = (_iota((rows, SB_SUB), 1) + c * SB_SUB) < (_iota((rows, SB_SUB), 0) + r0)
                log_keep = jnp.where(visible, log_keep, 0.0)
            carry = carry_ref[r0:tq, :]
            after = _rev_excl_cumsum_lanes(log_keep, upper) + carry
            w = jnp.exp(z - sp + after)
            if diagonal:
                w = jnp.where(visible, w, 0.0)
            acc_ref[r0:tq, :] += _dot(w.astype(BF16), v)
            carry_ref[r0:tq, :] = carry + jnp.sum(log_keep, axis=1, keepdims=True)

    @pl.when(kj == qi)
    def _():
        sweep(True)

    @pl.when(kj != qi)
    def _():
        sweep(False)

    @pl.when(kj == 0)
    def _():
        o_ref[...] = acc_ref[...].astype(o_ref.dtype)


def _sb_prompt(proj3, bias_lanes):
    b, seq, _ = proj3.shape
    nq = seq // SB_TQ
    q_of, k_of = [], []
    for qi in range(nq):
        for kj in range(qi, -1, -1):
            q_of.append(qi)
            k_of.append(kj)
    q_tab = jnp.asarray(q_of, jnp.int32)
    k_tab = jnp.asarray(k_of, jnp.int32)
    kb = COL_SBK // HEAD_DIM
    vb = COL_SBV // HEAD_DIM
    grid_spec = pltpu.PrefetchScalarGridSpec(
        num_scalar_prefetch=2,
        grid=(b, SB_HEADS, len(q_of)),
        in_specs=[
            pl.BlockSpec((None, SB_TQ, HEAD_DIM), lambda bi, h, p, qt, kt: (bi, qt[p], h)),
            pl.BlockSpec((None, SB_TQ, HEAD_DIM), lambda bi, h, p, qt, kt: (bi, kt[p], kb + h)),
            pl.BlockSpec((None, SB_TQ, HEAD_DIM), lambda bi, h, p, qt, kt: (bi, kt[p], vb + h)),
            pl.BlockSpec((None, 1, LANES), lambda bi, h, p, qt, kt: (h, 0, 0)),
        ],
        out_specs=pl.BlockSpec((None, SB_TQ, HEAD_DIM), lambda bi, h, p, qt, kt: (bi, qt[p], h)),
        scratch_shapes=[pltpu.VMEM((SB_TQ, HEAD_DIM), F32), pltpu.VMEM((SB_TQ, LANES), F32)],
    )
    return pl.pallas_call(
        _sb_prompt_kernel,
        grid_spec=grid_spec,
        out_shape=jax.ShapeDtypeStruct((b, seq, SB_WIDTH), BF16),
        compiler_params=_cparams(("parallel", "parallel", "arbitrary")),
        name="sb_prompt",
    )(q_tab, k_tab, proj3, proj3, proj3, bias_lanes)


def _sb_decode_kernel(pt_ref, q_ref, bias_ref, *refs):
    k_refs = refs[:DEC_PAGES]
    v_refs = refs[DEC_PAGES:2 * DEC_PAGES]
    o_ref, acc_ref, carry_ref = refs[2 * DEC_PAGES:]
    g = pl.program_id(1)
    scale = HEAD_DIM ** -0.5
    rows = PAGE * SB_HEADS
    upper = (_iota((LANES, LANES), 0) > _iota((LANES, LANES), 1)).astype(BF16)
    own_head = jnp.bitwise_and(_iota((SB_HEADS, rows), 1), SB_HEADS - 1) == _iota((SB_HEADS, rows), 0)

    @pl.when(g == 0)
    def _():
        acc_ref[...] = jnp.zeros_like(acc_ref)
        carry_ref[...] = jnp.zeros_like(carry_ref)

    q = q_ref[...].astype(BF16)
    bias = bias_ref[...]
    for p in reversed(range(DEC_PAGES)):
        k = k_refs[p][...].astype(BF16)
        v = v_refs[p][...].astype(BF16)
        z = _dot_nt(q, k) * scale + bias
        sp = _softplus(z)
        log_keep = jnp.where(own_head, -sp, 0.0)
        n_chunks = rows // LANES
        stacked = jnp.concatenate([log_keep[:, c * LANES:(c + 1) * LANES] for c in range(n_chunks)], axis=0)
        hi = stacked.astype(BF16)
        lo = (stacked - hi.astype(F32)).astype(BF16)
        cum = _dot(jnp.concatenate([hi, lo], axis=0), upper)
        cum = cum[:n_chunks * SB_HEADS] + cum[n_chunks * SB_HEADS:]
        total = jnp.sum(stacked, axis=1, keepdims=True)
        run = carry_ref[...]
        after = [None] * n_chunks
        for c in reversed(range(n_chunks)):
            after[c] = cum[c * SB_HEADS:(c + 1) * SB_HEADS] + run
            run = run + total[c * SB_HEADS:(c + 1) * SB_HEADS]
        carry_ref[...] = run
        w = jnp.where(own_head, jnp.exp(z - sp + jnp.concatenate(after, axis=1)), 0.0)
        acc_ref[...] += _dot(w.astype(BF16), v)

    @pl.when(g == pl.num_programs(1) - 1)
    def _():
        o_ref[...] = acc_ref[...]


def _sb_decode(q_heads, bias_rows, cache_k3, cache_v3, page_table, layer, n_phys):
    nb = q_heads.shape[0]
    rows = PAGE * SB_HEADS
    n_pages = page_table.shape[1]
    n_groups = n_pages // DEC_PAGES
    base = layer * n_phys

    def page_map(p):
        def index_map(bi, g, pt):
            return (base + pt[bi, (n_groups - 1 - g) * DEC_PAGES + p], 0, 0)
        return index_map

    page_specs = [pl.BlockSpec((None, rows, HEAD_DIM), page_map(p)) for p in range(DEC_PAGES)]
    grid_spec = pltpu.PrefetchScalarGridSpec(
        num_scalar_prefetch=1,
        grid=(nb, n_groups),
        in_specs=[pl.BlockSpec((None, SB_HEADS, HEAD_DIM), lambda bi, g, pt: (bi, 0, 0)),
                  pl.BlockSpec((SB_HEADS, rows), lambda bi, g, pt: (0, 0))] + page_specs + page_specs,
        out_specs=pl.BlockSpec((None, SB_HEADS, HEAD_DIM), lambda bi, g, pt: (bi, 0, 0)),
        scratch_shapes=[pltpu.VMEM((SB_HEADS, HEAD_DIM), F32), pltpu.VMEM((SB_HEADS, LANES), F32)],
    )
    return pl.pallas_call(
        _sb_decode_kernel,
        grid_spec=grid_spec,
        out_shape=jax.ShapeDtypeStruct((nb, SB_HEADS, HEAD_DIM), F32),
        compiler_params=_cparams(("parallel", "arbitrary")),
        name="sb_decode",
    )(page_table, q_heads, bias_rows, *([cache_k3] * DEC_PAGES), *([cache_v3] * DEC_PAGES))


def _unit_lower_inverse(n_strict):
    eye = (_iota((CHUNK, CHUNK), 0) == _iota((CHUNK, CHUNK), 1)).astype(F32)
    m = -n_strict
    t = eye + m
    power = 2
    while power < CHUNK:
        m = _dot(m, m)
        t = t + _dot(t, m)
        power *= 2
    return t


def _gdn_prompt_kernel(x_ref, z_ref, misc_ref, convw_ref, par_ref, nw_ref, o_ref, s_ref,
                       xbuf, ybuf, gbuf, bbuf):
    l = pl.program_id(1)
    lb = x_ref.shape[0]
    halo = 8

    @pl.when(l == 0)
    def _():
        s_ref[...] = jnp.zeros_like(s_ref)
        xbuf[0:halo, :] = jnp.zeros((halo, CONV_CH), F32)

    @pl.when(l != 0)
    def _():
        xbuf[0:halo, :] = xbuf[lb:lb + halo, :]

    xbuf[halo:halo + lb, :] = x_ref[...]
    y = jnp.zeros((lb, CONV_CH), F32)
    for i in range(GDN_CONV):
        off = halo - (GDN_CONV - 1) + i
        y = y + xbuf[off:off + lb, :] * convw_ref[i:i + 1, :]
    ybuf[...] = _silu(y)

    misc = misc_ref[...]
    neg_rate = -jnp.exp(par_ref[0:1, :])
    gbuf[...] = neg_rate * _softplus(misc + par_ref[1:2, :])
    bbuf[...] = _sigmoid(misc)

    row = _iota((CHUNK, CHUNK), 0)
    col = _iota((CHUNK, CHUNK), 1)
    lower = row >= col
    strict = row > col
    eye = (row == col).astype(F32)
    lower_f = lower.astype(F32)
    nw = nw_ref[...]

    def chunk_body(c, carry):
        r = pl.multiple_of(c * CHUNK, CHUNK)
        gcum = _dot(lower_f, gbuf[pl.ds(r, CHUNK), :])
        beta_all = bbuf[pl.ds(r, CHUNK), :]
        for h in range(GDN_HEADS):
            q = ybuf[pl.ds(r, CHUNK), h * GDN_DK:(h + 1) * GDN_DK]
            k = ybuf[pl.ds(r, CHUNK), (GDN_HEADS + h) * GDN_DK:(GDN_HEADS + h + 1) * GDN_DK]
            v = ybuf[pl.ds(r, CHUNK), 2 * GDN_HEADS * GDN_DK + h * GDN_DV:
                     2 * GDN_HEADS * GDN_DK + (h + 1) * GDN_DV]
            q = q * lax.rsqrt(jnp.sum(q * q, axis=-1, keepdims=True) + RMS_EPS) * (GDN_DK ** -0.5)
            k = k * lax.rsqrt(jnp.sum(k * k, axis=-1, keepdims=True) + RMS_EPS)
            beta = beta_all[:, MISC_BETA + h:MISC_BETA + h + 1]
            gcol = gcum[:, MISC_A + h:MISC_A + h + 1]
            grow = jnp.sum(eye * gcol, axis=0, keepdims=True)
            decay = jnp.where(lower, jnp.exp(jnp.minimum(gcol - grow, 0.0)), 0.0)
            kb = k * beta
            k16 = k.astype(BF16)
            kk = _dot_nt(kb.astype(BF16), k16) * decay
            t_inv = _unit_lower_inverse(jnp.where(strict, kk, 0.0))
            e_g = jnp.exp(gcol)
            u = _dot(t_inv, v * beta)
            w = _dot(t_inv, kb * e_g)
            intra = _dot_nt(q.astype(BF16), k16) * decay
            s = s_ref[h]
            s16 = s.astype(BF16)
            v_new = u - _dot(w.astype(BF16), s16)
            o = _dot((q * e_g).astype(BF16), s16) + _dot(intra.astype(BF16), v_new.astype(BF16))
            g_last = gcol[CHUNK - 1:CHUNK, :]
            s_ref[h] = s * jnp.exp(g_last) + _dot_tn(k * jnp.exp(g_last - gcol), v_new)
            zg = z_ref[pl.ds(r, CHUNK), h * GDN_DV:(h + 1) * GDN_DV]
            o_ref[pl.ds(r, CHUNK), h * GDN_DV:(h + 1) * GDN_DV] = _gated_rms(o, zg, nw).astype(o_ref.dtype)
        return carry

    lax.fori_loop(0, lb // CHUNK, chunk_body, 0)


GDN_LB = 1024


def _gdn_prompt(proj3, conv_w, par, nw):
    b, seq, _ = proj3.shape
    lb = GDN_LB
    return pl.pallas_call(
        _gdn_prompt_kernel,
        grid=(b, seq // lb),
        in_specs=[
            pl.BlockSpec((None, lb, CONV_CH), lambda bi, l: (bi, l, COL_CONV // CONV_CH)),
            pl.BlockSpec((None, lb, GDN_WIDTH), lambda bi, l: (bi, l, COL_GZ // GDN_WIDTH)),
            pl.BlockSpec((None, lb, LANES), lambda bi, l: (bi, l, COL_MISC // LANES)),
            pl.BlockSpec((GDN_CONV, CONV_CH), lambda bi, l: (0, 0)),
            pl.BlockSpec((8, LANES), lambda bi, l: (0, 0)),
            pl.BlockSpec((1, GDN_DV), lambda bi, l: (0, 0)),
        ],
        out_specs=[
            pl.BlockSpec((None, lb, GDN_WIDTH), lambda bi, l: (bi, l, 0)),
            pl.BlockSpec((None, GDN_HEADS, GDN_DK, GDN_DV), lambda bi, l: (bi, 0, 0, 0)),
        ],
        out_shape=[jax.ShapeDtypeStruct((b, seq, GDN_WIDTH), BF16),
                   jax.ShapeDtypeStruct((b, GDN_HEADS, GDN_DK, GDN_DV), F32)],
        scratch_shapes=[pltpu.VMEM((lb + 8, CONV_CH), F32), pltpu.VMEM((lb, CONV_CH), F32),
                        pltpu.VMEM((lb, LANES), F32), pltpu.VMEM((lb, LANES), F32)],
        compiler_params=_cparams(("parallel", "arbitrary")),
        name="gdn_prompt",
    )(proj3, proj3, proj3, conv_w, par, nw)


def _gla_prompt_kernel(x_ref, gate_ref, misc_ref, wg_ref, bg_ref, nw_ref, o_ref, s_ref, abuf):
    l = pl.program_id(1)
    lb = x_ref.shape[0]
    kw = GLA_HEADS * GLA_DK

    @pl.when(l == 0)
    def _():
        s_ref[...] = jnp.zeros_like(s_ref)

    abuf[...] = _log_sigmoid(_dot(misc_ref[...], wg_ref[...]) + bg_ref[...]) * (1.0 / GLA_TAU)

    row = _iota((CHUNK, CHUNK), 0)
    col = _iota((CHUNK, CHUNK), 1)
    lower = row >= col
    lower_f = lower.astype(F32)
    eye_k = (_iota((GLA_DK, GLA_DK), 0) == _iota((GLA_DK, GLA_DK), 1)).astype(F32)
    key_row = _iota((CHUNK, GLA_DK), 0)
    nw = nw_ref[...]

    def chunk_body(c, carry):
        r = pl.multiple_of(c * CHUNK, CHUNK)
        bcum = _dot(lower_f, abuf[pl.ds(r, CHUNK), :])
        for h in range(GLA_HEADS):
            q = x_ref[pl.ds(r, CHUNK), h * GLA_DK:(h + 1) * GLA_DK] * (GLA_DK ** -0.5)
            k = x_ref[pl.ds(r, CHUNK), kw + h * GLA_DK:kw + (h + 1) * GLA_DK]
            v = x_ref[pl.ds(r, CHUNK), 2 * kw + h * GLA_DV:2 * kw + (h + 1) * GLA_DV]
            bh = bcum[:, h * GLA_DK:(h + 1) * GLA_DK]
            pieces = []
            for i in range(CHUNK // GLA_SUB):
                lo, hi = i * GLA_SUB, (i + 1) * GLA_SUB
                ref = bh[lo:lo + 1, :]
                qn = q[lo:hi, :] * jnp.exp(bh[lo:hi, :] - ref)
                kn = k * jnp.exp(jnp.where(key_row < hi, ref - bh, 0.0))
                pieces.append(_dot_nt(qn.astype(BF16), kn.astype(BF16)))
            att = jnp.where(lower, jnp.concatenate(pieces, axis=0), 0.0)
            s = s_ref[h]
            o = _dot((q * jnp.exp(bh)).astype(BF16), s.astype(BF16)) + _dot(att.astype(BF16), v.astype(BF16))
            b_last = bh[CHUNK - 1:CHUNK, :]
            s_ref[h] = _dot(eye_k * jnp.exp(b_last), s) + _dot_tn(k * jnp.exp(b_last - bh), v)
            zg = gate_ref[pl.ds(r, CHUNK), h * GLA_DV:(h + 1) * GLA_DV]
            o_ref[pl.ds(r, CHUNK), h * GLA_DV:(h + 1) * GLA_DV] = _gated_rms(o, zg, nw).astype(o_ref.dtype)
        return carry

    lax.fori_loop(0, lb // CHUNK, chunk_body, 0)


GLA_LB = 1024


def _gla_prompt(proj3, wg_pad, bg, nw):
    b, seq, _ = proj3.shape
    lb = GLA_LB
    xw = 2 * GLA_HEADS * GLA_DK + GLA_WIDTH
    return pl.pallas_call(
        _gla_prompt_kernel,
        grid=(b, seq // lb),
        in_specs=[
            pl.BlockSpec((None, lb, xw), lambda bi, l: (bi, l, COL_LQKV // xw)),
            pl.BlockSpec((None, lb, GLA_WIDTH), lambda bi, l: (bi, l, COL_LG // GLA_WIDTH)),
            pl.BlockSpec((None, lb, LANES), lambda bi, l: (bi, l, COL_MISC // LANES)),
            pl.BlockSpec((LANES, GLA_HEADS * GLA_DK), lambda bi, l: (0, 0)),
            pl.BlockSpec((1, GLA_HEADS * GLA_DK), lambda bi, l: (0, 0)),
            pl.BlockSpec((1, GLA_DV), lambda bi, l: (0, 0)),
        ],
        out_specs=[
            pl.BlockSpec((None, lb, GLA_WIDTH), lambda bi, l: (bi, l, 0)),
            pl.BlockSpec((None, GLA_HEADS, GLA_DK, GLA_DV), lambda bi, l: (bi, 0, 0, 0)),
        ],
        out_shape=[jax.ShapeDtypeStruct((b, seq, GLA_WIDTH), BF16),
                   jax.ShapeDtypeStruct((b, GLA_HEADS, GLA_DK, GLA_DV), F32)],
        scratch_shapes=[pltpu.VMEM((lb, GLA_HEADS * GLA_DK), F32)],
        compiler_params=_cparams(("parallel", "arbitrary")),
        name="gla_prompt",
    )(proj3, proj3, proj3, wg_pad, bg, nw)


def _to_column(row, n):
    eye = (_iota((n, n), 0) == _iota((n, n), 1)).astype(F32)
    return jnp.sum(eye * row, axis=1, keepdims=True)


def _recurrent_step_kernel(x_ref, sg_ref, sc_ref, sl_ref, convw_ref, par_ref, gnw_ref,
                           wg_ref, bg_ref, lnw_ref, og_ref, ol_ref, sg_out, sc_out, sl_out):
    xrow = x_ref[...]
    new = xrow[:, COL_CONV:COL_CONV + CONV_CH]
    old = sc_ref[...]
    sc_out[...] = jnp.concatenate([old[1:GDN_CONV - 1, :], new], axis=0)
    y = new * convw_ref[GDN_CONV - 1:GDN_CONV, :]
    for i in range(GDN_CONV - 1):
        y = y + old[i:i + 1, :] * convw_ref[i:i + 1, :]
    y = _silu(y)

    misc = xrow[:, COL_MISC:COL_MISC + LANES]
    g_all = -jnp.exp(par_ref[0:1, :]) * _softplus(misc + par_ref[1:2, :])
    beta_all = _sigmoid(misc)
    gnw = gnw_ref[...]
    for h in range(GDN_HEADS):
        q = y[:, h * GDN_DK:(h + 1) * GDN_DK]
        k = y[:, (GDN_HEADS + h) * GDN_DK:(GDN_HEADS + h + 1) * GDN_DK]
        v = y[:, 2 * GDN_HEADS * GDN_DK + h * GDN_DV:2 * GDN_HEADS * GDN_DK + (h + 1) * GDN_DV]
        q = q * lax.rsqrt(jnp.sum(q * q, axis=-1, keepdims=True) + RMS_EPS) * (GDN_DK ** -0.5)
        k = k * lax.rsqrt(jnp.sum(k * k, axis=-1, keepdims=True) + RMS_EPS)
        beta = beta_all[:, MISC_BETA + h:MISC_BETA + h + 1]
        e_g = jnp.exp(g_all[:, MISC_A + h:MISC_A + h + 1])
        s = sg_ref[h]
        kcol = _to_column(k, GDN_DK)
        qcol = _to_column(q, GDN_DK)
        v_new = beta * (v - e_g * jnp.sum(s * kcol, axis=0, keepdims=True))
        qk = jnp.sum(q * k, axis=-1, keepdims=True)
        o = e_g * jnp.sum(s * qcol, axis=0, keepdims=True) + qk * v_new
        sg_out[h] = s * e_g + kcol * v_new
        zg = xrow[:, COL_GZ + h * GDN_DV:COL_GZ + (h + 1) * GDN_DV]
        og_ref[:, h * GDN_DV:(h + 1) * GDN_DV] = _gated_rms(o, zg, gnw).astype(og_ref.dtype)

    kw = GLA_HEADS * GLA_DK
    log_a = _log_sigmoid(_dot(misc, wg_ref[...]) + bg_ref[...]) * (1.0 / GLA_TAU)
    lnw = lnw_ref[...]
    for h in range(GLA_HEADS):
        q = xrow[:, COL_LQKV + h * GLA_DK:COL_LQKV + (h + 1) * GLA_DK] * (GLA_DK ** -0.5)
        k = xrow[:, COL_LQKV + kw + h * GLA_DK:COL_LQKV + kw + (h + 1) * GLA_DK]
        v = xrow[:, COL_LQKV + 2 * kw + h * GLA_DV:COL_LQKV + 2 * kw + (h + 1) * GLA_DV]
        a = jnp.exp(log_a[:, h * GLA_DK:(h + 1) * GLA_DK])
        s = sl_ref[h]
        qk = jnp.sum(q * k, axis=-1, keepdims=True)
        o = jnp.sum(s * _to_column(q * a, GLA_DK), axis=0, keepdims=True) + qk * v
        sl_out[h] = _to_column(a, GLA_DK) * s + _to_column(k, GLA_DK) * v
        zg = xrow[:, COL_LG + h * GLA_DV:COL_LG + (h + 1) * GLA_DV]
        ol_ref[:, h * GLA_DV:(h + 1) * GLA_DV] = _gated_rms(o, zg, lnw).astype(ol_ref.dtype)


def _recurrent_step(proj_s3, state_gdn, state_conv, state_gla, layer, conv_w, par, gnw, wg_pad, bg, lnw):
    nb = proj_s3.shape[0]
    const2 = lambda bi: (0, 0)
    return pl.pallas_call(
        _recurrent_step_kernel,
        grid=(nb,),
        in_specs=[
            pl.BlockSpec((None, 1, N_PROJ), lambda bi: (bi, 0, 0)),
            pl.BlockSpec((None, None, GDN_HEADS, GDN_DK, GDN_DV), lambda bi: (layer, bi, 0, 0, 0)),
            pl.BlockSpec((None, None, GDN_CONV - 1, CONV_CH), lambda bi: (layer, bi, 0, 0)),
            pl.BlockSpec((None, None, GLA_HEADS, GLA_DK, GLA_DV), lambda bi: (layer, bi, 0, 0, 0)),
            pl.BlockSpec((GDN_CONV, CONV_CH), const2),
            pl.BlockSpec((8, LANES), const2),
            pl.BlockSpec((1, GDN_DV), const2),
            pl.BlockSpec((LANES, GLA_HEADS * GLA_DK), const2),
            pl.BlockSpec((1, GLA_HEADS * GLA_DK), const2),
            pl.BlockSpec((1, GLA_DV), const2),
        ],
        out_specs=[
            pl.BlockSpec((None, 1, GDN_WIDTH), lambda bi: (bi, 0, 0)),
            pl.BlockSpec((None, 1, GLA_WIDTH), lambda bi: (bi, 0, 0)),
            pl.BlockSpec((None, GDN_HEADS, GDN_DK, GDN_DV), lambda bi: (bi, 0, 0, 0)),
            pl.BlockSpec((None, GDN_CONV - 1, CONV_CH), lambda bi: (bi, 0, 0)),
            pl.BlockSpec((None, GLA_HEADS, GLA_DK, GLA_DV), lambda bi: (bi, 0, 0, 0)),
        ],
        out_shape=[
            jax.ShapeDtypeStruct((nb, 1, GDN_WIDTH), F32),
            jax.ShapeDtypeStruct((nb, 1, GLA_WIDTH), F32),
            jax.ShapeDtypeStruct((nb, GDN_HEADS, GDN_DK, GDN_DV), F32),
            jax.ShapeDtypeStruct((nb, GDN_CONV - 1, CONV_CH), F32),
            jax.ShapeDtypeStruct((nb, GLA_HEADS, GLA_DK, GLA_DV), F32),
        ],
        compiler_params=_cparams(("parallel",)),
        name="recurrent_step",
    )(proj_s3, state_gdn, state_conv, state_gla, conv_w, par, gnw, wg_pad, bg, lnw)


def _mix_out_kernel(x_ref, osb_ref, ogdn_ref, ogla_ref, w_ref, g_ref, b_ref, rw_ref, rb_ref,
                    x1_ref, x1h_ref, logit_ref, *, alpha):
    mixed = (_dot(osb_ref[...].astype(BF16), w_ref[0:SB_WIDTH, :])
             + _dot(ogdn_ref[...].astype(BF16), w_ref[SB_WIDTH:SB_WIDTH + GDN_WIDTH, :])
             + _dot(ogla_ref[...].astype(BF16), w_ref[SB_WIDTH + GDN_WIDTH:, :]))
    x1 = _layer_norm(alpha * x_ref[...] + mixed, g_ref[...], b_ref[...])
    x1_ref[...] = x1
    x1h_ref[...] = x1.astype(BF16)
    logit_ref[...] = _dot(x1, rw_ref[...]) + rb_ref[...]


def _mix_out(x, o_sb, o_gdn, o_gla, w_out, ln_g, ln_b, router_w, router_b, alpha, tm):
    m, d = x.shape
    row = lambda i: (i, 0)
    const = lambda i: (0, 0)
    return pl.pallas_call(
        functools.partial(_mix_out_kernel, alpha=alpha),
        grid=(m // tm,),
        in_specs=[
            pl.BlockSpec((tm, d), row),
            pl.BlockSpec((tm, SB_WIDTH), row),
            pl.BlockSpec((tm, GDN_WIDTH), row),
            pl.BlockSpec((tm, GLA_WIDTH), row),
            pl.BlockSpec((d, d), const),
            pl.BlockSpec((1, d), const),
            pl.BlockSpec((1, d), const),
            pl.BlockSpec((d, LANES), const),
            pl.BlockSpec((1, LANES), const),
        ],
        out_specs=[pl.BlockSpec((tm, d), row), pl.BlockSpec((tm, d), row), pl.BlockSpec((tm, LANES), row)],
        out_shape=[jax.ShapeDtypeStruct((m, d), F32), jax.ShapeDtypeStruct((m, d), BF16),
                   jax.ShapeDtypeStruct((m, LANES), F32)],
        compiler_params=_cparams(("parallel",)),
        name="mix_out_ln_router",
    )(x, o_sb, o_gdn, o_gla, w_out, ln_g, ln_b, router_w, router_b)


def _expert_changed(be_ref, i):
    prev = be_ref[jnp.maximum(i - 1, 0)]
    return jnp.logical_or(i == 0, be_ref[i] != prev)


def _moe_up_kernel(be_ref, x_ref, wg_ref, wu_ref, bg_ref, bu_ref, h_ref, wg_bf, wu_bf):
    i = pl.program_id(1)

    @pl.when(_expert_changed(be_ref, i))
    def _():
        wg_bf[...] = wg_ref[...].astype(BF16)
        wu_bf[...] = wu_ref[...].astype(BF16)

    x = x_ref[...]
    gate = jnp.minimum(_dot(x, wg_bf[...]) + bg_ref[...], SWIGLU_LIMIT)
    up = jnp.clip(_dot(x, wu_bf[...]) + bu_ref[...], -SWIGLU_LIMIT, SWIGLU_LIMIT)
    h_ref[...] = ((up + 1.0) * (gate * _sigmoid(SWIGLU_ALPHA * gate))).astype(h_ref.dtype)


def _moe_down_kernel(be_ref, h_ref, w_ref, b_ref, gate_ref, y_ref, w_bf):
    i = pl.program_id(1)

    @pl.when(_expert_changed(be_ref, i))
    def _():
        w_bf[...] = w_ref[...].astype(BF16)

    y_ref[...] = (_dot(h_ref[...], w_bf[...]) + b_ref[...]) * gate_ref[...]


def _moe_experts(x_rows, row_gate, blk_expert, w_gu, b_gu4, w_dn, b_dn4, layer):
    n_rows, d = x_rows.shape
    n_blocks = n_rows // MOE_BLK
    nj = MOE_FF // MOE_TN
    up_spec = pltpu.PrefetchScalarGridSpec(
        num_scalar_prefetch=1,
        grid=(nj, n_blocks),
        in_specs=[
            pl.BlockSpec((MOE_BLK, d), lambda j, i, be: (i, 0)),
            pl.BlockSpec((None, None, d, MOE_TN), lambda j, i, be: (layer, be[i], 0, j)),
            pl.BlockSpec((None, None, d, MOE_TN), lambda j, i, be: (layer, be[i], 0, nj + j)),
            pl.BlockSpec((None, None, 1, MOE_TN), lambda j, i, be: (layer, be[i], 0, j)),
            pl.BlockSpec((None, None, 1, MOE_TN), lambda j, i, be: (layer, be[i], 0, nj + j)),
        ],
        out_specs=pl.BlockSpec((MOE_BLK, MOE_TN), lambda j, i, be: (i, j)),
        scratch_shapes=[pltpu.VMEM((d, MOE_TN), BF16), pltpu.VMEM((d, MOE_TN), BF16)],
    )
    hidden = pl.pallas_call(
        _moe_up_kernel,
        grid_spec=up_spec,
        out_shape=jax.ShapeDtypeStruct((n_rows, MOE_FF), BF16),
        compiler_params=_cparams(("arbitrary", "arbitrary")),
        name="moe_up",
    )(blk_expert, x_rows, w_gu, w_gu, b_gu4, b_gu4)

    nj_dn = d // MOE_TN
    down_spec = pltpu.PrefetchScalarGridSpec(
        num_scalar_prefetch=1,
        grid=(nj_dn, n_blocks),
        in_specs=[
            pl.BlockSpec((MOE_BLK, MOE_FF), lambda j, i, be: (i, 0)),
            pl.BlockSpec((None, None, MOE_FF, MOE_TN), lambda j, i, be: (layer, be[i], 0, j)),
            pl.BlockSpec((None, None, 1, MOE_TN), lambda j, i, be: (layer, be[i], 0, j)),
            pl.BlockSpec((MOE_BLK, 1), lambda j, i, be: (i, 0)),
        ],
        out_specs=pl.BlockSpec((MOE_BLK, MOE_TN), lambda j, i, be: (i, j)),
        scratch_shapes=[pltpu.VMEM((MOE_FF, MOE_TN), BF16)],
    )
    return pl.pallas_call(
        _moe_down_kernel,
        grid_spec=down_spec,
        out_shape=jax.ShapeDtypeStruct((n_rows, d), F32),
        compiler_params=_cparams(("arbitrary", "arbitrary")),
        name="moe_down",
    )(blk_expert, hidden, w_dn, b_dn4, row_gate)


def _route(logits):
    t = logits.shape[0]
    top_val, top_idx = lax.top_k(logits, TOP_K)
    gates = jax.nn.softmax(top_val, axis=-1)
    n_assign = t * TOP_K
    flat_e = top_idx.reshape(n_assign).astype(jnp.int32)
    flat_gate = gates.reshape(n_assign)
    order = jnp.argsort(flat_e).astype(jnp.int32)
    rank = jnp.argsort(order).astype(jnp.int32)
    counts = jnp.bincount(flat_e, length=N_EXPERTS).astype(jnp.int32)
    padded = (counts + MOE_BLK - 1) // MOE_BLK * MOE_BLK
    pad_end = jnp.cumsum(padded)
    pad_start = pad_end - padded
    start = jnp.cumsum(counts) - counts
    n_blocks = -(-(n_assign + N_EXPERTS * (MOE_BLK - 1)) // MOE_BLK)
    blk_first = jnp.arange(n_blocks, dtype=jnp.int32) * MOE_BLK
    blk_expert = jnp.minimum(
        jnp.sum((pad_end[None, :] <= blk_first[:, None]).astype(jnp.int32), axis=1), N_EXPERTS - 1)
    pos = (jnp.arange(n_blocks, dtype=jnp.int32) * MOE_BLK - pad_start[blk_expert])[:, None] \
        + jnp.arange(MOE_BLK, dtype=jnp.int32)[None, :]
    live = pos < counts[blk_expert][:, None]
    src = jnp.where(live, start[blk_expert][:, None] + pos, 0).reshape(-1)
    assign = order[src]
    live = live.reshape(-1)
    row_tok = jnp.where(live, assign // TOP_K, 0)
    row_gate = jnp.where(live, flat_gate[assign], 0.0)
    dest = ((pad_start - start)[flat_e] + rank).reshape(t, TOP_K)
    return row_tok, row_gate, blk_expert, dest


def _ffn_out_kernel(x1_ref, moe_ref, p_ref, wp_ref, wg_ref, g_ref, b_ref, x3_ref, *, alpha):
    g = g_ref[...]
    b = b_ref[...]
    x2 = _layer_norm(alpha * x1_ref[...] + moe_ref[...], g[0:1, :], b[0:1, :])
    ple = _dot(p_ref[...].astype(BF16), wp_ref[...]) * _sigmoid(_dot(x2.astype(BF16), wg_ref[...]))
    x3_ref[...] = _layer_norm(alpha * x2 + ple, g[1:2, :], b[1:2, :])


def _ffn_out(x1, moe, p, wp, wg, ln_g2, ln_b2, alpha, tm):
    m, d = x1.shape
    pd = p.shape[1]
    row = lambda i: (i, 0)
    const = lambda i: (0, 0)
    return pl.pallas_call(
        functools.partial(_ffn_out_kernel, alpha=alpha),
        grid=(m // tm,),
        in_specs=[
            pl.BlockSpec((tm, d), row),
            pl.BlockSpec((tm, d), row),
            pl.BlockSpec((tm, pd), row),
            pl.BlockSpec((pd, d), const),
            pl.BlockSpec((d, d), const),
            pl.BlockSpec((2, d), const),
            pl.BlockSpec((2, d), const),
        ],
        out_specs=pl.BlockSpec((tm, d), row),
        out_shape=jax.ShapeDtypeStruct((m, d), F32),
        compiler_params=_cparams(("parallel",)),
        name="ffn_out_ple_ln",
    )(x1, moe, p, wp, wg, ln_g2, ln_b2)


def _permute_w_in(w):
    c = 5 * SB_WIDTH
    g_b = w[:, c:c + GDN_HEADS]
    g_a = w[:, c + GDN_HEADS:c + 2 * GDN_HEADS]
    c += 2 * GDN_HEADS
    kw = GLA_HEADS * GLA_DK
    l_q = w[:, c:c + kw]
    l_k = w[:, c + kw:c + 2 * kw]
    l_v = w[:, c + 2 * kw:c + 2 * kw + GLA_WIDTH]
    c += 2 * kw + GLA_WIDTH
    l_r = w[:, c:c + GLA_RANK]
    l_g = w[:, c + GLA_RANK:c + GLA_RANK + GLA_WIDTH]
    used = COL_MISC + 2 * GDN_HEADS + GLA_RANK
    pad = jnp.zeros((w.shape[0], N_PROJ - used), w.dtype)
    return jnp.concatenate([w[:, :5 * SB_WIDTH], l_q, l_k, l_v, l_g, g_b, g_a, l_r, pad], axis=1).astype(BF16)


def kernel(x_prompt, x_sample, cache_k, cache_v, state_gdn, state_conv, state_gla, page_table, p_prompt, p_sample, w_in, sb_logit_bias, gdn_conv_w, gdn_a_log, gdn_dt_bias, gdn_norm_w, gla_w_gate, gla_b_gate, gla_norm_w, w_out, ln_g, ln_b, router_w, router_b, moe_w_gu, moe_b_gu, moe_w_dn, moe_b_dn, ple_w_proj, ple_w_gate):
    depth = w_in.shape[0]
    b, seq, d = x_prompt.shape
    nb = x_sample.shape[0]
    t_p = b * seq
    n_phys = cache_k.shape[1]
    alpha = (2 * depth) ** 0.25

    cache_k3 = cache_k.reshape(depth * n_phys, PAGE * SB_HEADS, HEAD_DIM)
    cache_v3 = cache_v.reshape(depth * n_phys, PAGE * SB_HEADS, HEAD_DIM)
    b_gu4 = moe_b_gu.reshape(depth, N_EXPERTS, 1, 2 * MOE_FF)
    b_dn4 = moe_b_dn.reshape(depth, N_EXPERTS, 1, d)

    h_p = x_prompt.reshape(t_p, d)
    h_s = x_sample.reshape(nb, d)
    outs_p = ([], [], [], [], [])
    outs_s = ([], [], [], [], [])
    for i in range(depth):
        w_in_i = _permute_w_in(w_in[i])
        w_out_i = w_out[i].astype(BF16)
        wp_i = ple_w_proj[i].astype(BF16)
        wgate_i = ple_w_gate[i].astype(BF16)
        bias_lanes = jnp.broadcast_to(sb_logit_bias[i][:, None, None], (SB_HEADS, 1, LANES))
        bias_rows = jnp.broadcast_to(sb_logit_bias[i][:, None], (SB_HEADS, PAGE * SB_HEADS))
        par = jnp.zeros((8, LANES), F32)
        par = par.at[0, MISC_A:MISC_A + GDN_HEADS].set(gdn_a_log[i])
        par = par.at[1, MISC_A:MISC_A + GDN_HEADS].set(gdn_dt_bias[i])
        gnw = gdn_norm_w[i].reshape(1, GDN_DV)
        lnw = gla_norm_w[i].reshape(1, GLA_DV)
        wg_pad = jnp.zeros((LANES, GLA_HEADS * GLA_DK), F32).at[MISC_R:MISC_R + GLA_RANK].set(gla_w_gate[i])
        bg = gla_b_gate[i].reshape(1, GLA_HEADS * GLA_DK)
        rw = jnp.zeros((d, LANES), F32).at[:, :N_EXPERTS].set(router_w[i])
        rb = jnp.zeros((1, LANES), F32).at[0, :N_EXPERTS].set(router_b[i])
        g0, b0 = ln_g[i, 0:1], ln_b[i, 0:1]

        proj_p = _project(h_p, w_in_i, 1024)
        proj_p3 = proj_p.reshape(b, seq, N_PROJ)
        o_sb = _sb_prompt(proj_p3, bias_lanes)
        o_gdn, gdn_state_p = _gdn_prompt(proj_p3, gdn_conv_w[i], par, gnw)
        o_gla, gla_state_p = _gla_prompt(proj_p3, wg_pad, bg, lnw)
        x1_p, x1h_p, logit_p = _mix_out(h_p, o_sb.reshape(t_p, SB_WIDTH), o_gdn.reshape(t_p, GDN_WIDTH),
                                        o_gla.reshape(t_p, GLA_WIDTH), w_out_i, g0, b0, rw, rb, alpha, 256)
        outs_p[0].append(proj_p3[:, :, COL_SBK:COL_SBK + SB_WIDTH].reshape(b, seq, SB_HEADS, HEAD_DIM))
        outs_p[1].append(proj_p3[:, :, COL_SBV:COL_SBV + SB_WIDTH].reshape(b, seq, SB_HEADS, HEAD_DIM))
        outs_p[2].append(gdn_state_p)
        outs_p[3].append(proj_p3[:, seq - (GDN_CONV - 1):, COL_CONV:COL_CONV + CONV_CH])
        outs_p[4].append(gla_state_p)

        proj_s = _project(h_s, w_in_i, nb)
        proj_s3 = proj_s.reshape(nb, 1, N_PROJ)
        q_heads_s = proj_s[:, COL_SBQ:COL_SBQ + SB_WIDTH].reshape(nb, SB_HEADS, HEAD_DIM)
        o_sb_s = _sb_decode(q_heads_s, bias_rows, cache_k3, cache_v3, page_table, i, n_phys)
        o_gdn_s, o_gla_s, gdn_state_s, conv_state_s, gla_state_s = _recurrent_step(
            proj_s3, state_gdn, state_conv, state_gla, i, gdn_conv_w[i], par, gnw, wg_pad, bg, lnw)
        x1_s, x1h_s, logit_s = _mix_out(h_s, o_sb_s.reshape(nb, SB_WIDTH), o_gdn_s.reshape(nb, GDN_WIDTH),
                                        o_gla_s.reshape(nb, GLA_WIDTH), w_out_i, g0, b0, rw, rb, alpha, nb)
        outs_s[0].append(proj_s[:, COL_SBK:COL_SBK + SB_WIDTH].reshape(nb, 1, SB_HEADS, HEAD_DIM))
        outs_s[1].append(proj_s[:, COL_SBV:COL_SBV + SB_WIDTH].reshape(nb, 1, SB_HEADS, HEAD_DIM))
        outs_s[2].append(gdn_state_s)
        outs_s[3].append(conv_state_s)
        outs_s[4].append(gla_state_s)

        logits = jnp.concatenate([logit_p[:, :N_EXPERTS], logit_s[:, :N_EXPERTS]], axis=0)
        row_tok, row_gate, blk_e, dest = _route(logits)
        x_rows = jnp.concatenate([x1h_p, x1h_s], axis=0)[row_tok]
        y_rows = _moe_experts(x_rows, row_gate[:, None], blk_e, moe_w_gu, b_gu4, moe_w_dn, b_dn4, i)
        moe_all = jnp.sum(y_rows[dest], axis=1)
        moe_p, moe_s = moe_all[:t_p], moe_all[t_p:]

        g12, b12 = ln_g[i, 1:3], ln_b[i, 1:3]
        h_p = _ffn_out(x1_p, moe_p, p_prompt[i].reshape(t_p, -1), wp_i, wgate_i, g12, b12, alpha, 256)
        h_s = _ffn_out(x1_s, moe_s, p_sample[i].reshape(nb, -1), wp_i, wgate_i, g12, b12, alpha, nb)

    y_prompt = h_p.reshape(b, seq, d)
    y_sample = h_s.reshape(nb, 1, d)
    return (y_prompt, y_sample,
            jnp.stack(outs_p[0]), jnp.stack(outs_p[1]), jnp.stack(outs_p[2]), jnp.stack(outs_p[3]),
            jnp.stack(outs_p[4]),
            jnp.stack(outs_s[0]), jnp.stack(outs_s[1]), jnp.stack(outs_s[2]), jnp.stack(outs_s[3]),
            jnp.stack(outs_s[4]))
```

```python
import functools

import jax
import jax.numpy as jnp
from jax import lax
from jax.experimental import pallas as pl
from jax.experimental.pallas import tpu as pltpu

F32 = jnp.float32
BF16 = jnp.bfloat16

D_MODEL = 2048
HEAD_DIM = 128
SB_HEADS = 8
GDN_HEADS = 4
GDN_DK = 128
GDN_DV = 128
GDN_CONV = 4
GLA_HEADS = 4
GLA_DK = 64
GLA_DV = 128
GLA_RANK = 16
GLA_TAU = 16.0
CHUNK = 64
GLA_SUB = 16
SB_WIDTH = SB_HEADS * HEAD_DIM
GDN_WIDTH = GDN_HEADS * GDN_DV
GLA_WIDTH = GLA_HEADS * GLA_DV
CONV_CH = 2 * GDN_HEADS * GDN_DK + GDN_WIDTH
N_EXPERTS = 32
TOP_K = 4
MOE_FF = D_MODEL
SWIGLU_LIMIT = 7.0
SWIGLU_ALPHA = 1.702
LN_EPS = 1e-5
RMS_EPS = 1e-6

COL_SBQ = 0
COL_SBK = 1024
COL_SBV = 2048
COL_CONV = 3072
COL_GZ = 4608
COL_LQKV = 5120
COL_LG = 6144
COL_MISC = 6656
MISC_BETA = 0
MISC_A = 4
MISC_R = 8
N_PROJ = 6912
PROJ_TN = 768

LANES = 128
VMEM_LIMIT = 56 * 1024 * 1024

MOE_BLK = 256
MOE_TN = 1024

SB_TQ = 1024
SB_SUB = 128
DEC_PAGES = 16
PAGE = 128


def _cparams(sem):
    return pltpu.CompilerParams(dimension_semantics=sem, vmem_limit_bytes=VMEM_LIMIT)


def _softplus(x):
    return jnp.maximum(x, 0.0) + jnp.log(1.0 + jnp.exp(-jnp.abs(x)))


def _log_sigmoid(x):
    return jnp.minimum(x, 0.0) - jnp.log(1.0 + jnp.exp(-jnp.abs(x)))


def _sigmoid(x):
    return 1.0 / (1.0 + jnp.exp(-x))


def _silu(x):
    return x * _sigmoid(x)


def _iota(shape, dim):
    return lax.broadcasted_iota(jnp.int32, shape, dim)


def _contract(a, b, dims):
    precision = lax.Precision.HIGHEST if (a.dtype == F32 and b.dtype == F32) else None
    return lax.dot_general(a, b, (dims, ((), ())), precision=precision, preferred_element_type=F32)


def _dot(a, b):
    return _contract(a, b, ((1,), (0,)))


def _dot_nt(a, b):
    return _contract(a, b, ((1,), (1,)))


def _dot_tn(a, b):
    return _contract(a, b, ((0,), (0,)))


def _rev_excl_cumsum_lanes(x, upper):
    hi = x.astype(BF16)
    lo = (x - hi.astype(F32)).astype(BF16)
    return _dot(hi, upper) + _dot(lo, upper)


def _layer_norm(h, g, b):
    mu = jnp.mean(h, axis=-1, keepdims=True)
    d = h - mu
    var = jnp.mean(d * d, axis=-1, keepdims=True)
    return d * lax.rsqrt(var + LN_EPS) * g + b


def _gated_rms(o, z, w):
    of = o * lax.rsqrt(jnp.mean(o * o, axis=-1, keepdims=True) + RMS_EPS) * w
    return of * _silu(z)


def _proj_kernel(x_ref, w_ref, o_ref):
    o_ref[...] = _dot(x_ref[...].astype(BF16), w_ref[...])


def _project(x, w, tm):
    m, k = x.shape
    n = w.shape[1]
    return pl.pallas_call(
        _proj_kernel,
        grid=(m // tm, n // PROJ_TN),
        in_specs=[pl.BlockSpec((tm, k), lambda i, j: (i, 0)),
                  pl.BlockSpec((k, PROJ_TN), lambda i, j: (0, j))],
        out_specs=pl.BlockSpec((tm, PROJ_TN), lambda i, j: (i, j)),
        out_shape=jax.ShapeDtypeStruct((m, n), F32),
        compiler_params=_cparams(("parallel", "arbitrary")),
        name="in_proj",
    )(x, w)


def _sb_prompt_kernel(qt_ref, kt_ref, q_ref, k_ref, v_ref, bias_ref, o_ref, acc_ref, carry_ref):
    p = pl.program_id(2)
    qi = qt_ref[p]
    kj = kt_ref[p]
    tq = q_ref.shape[0]
    nsub = k_ref.shape[0] // SB_SUB
    scale = HEAD_DIM ** -0.5
    bias = bias_ref[...]
    upper = (_iota((SB_SUB, SB_SUB), 0) > _iota((SB_SUB, SB_SUB), 1)).astype(BF16)

    @pl.when(kj == qi)
    def _():
        acc_ref[...] = jnp.zeros_like(acc_ref)
        carry_ref[...] = jnp.zeros_like(carry_ref)

    def sweep(diagonal):
        for c in reversed(range(nsub)):
            r0 = c * SB_SUB if diagonal else 0
            rows = tq - r0
            q = q_ref[r0:tq, :].astype(BF16)
            k = k_ref[c * SB_SUB:(c + 1) * SB_SUB, :].astype(BF16)
            v = v_ref[c * SB_SUB:(c + 1) * SB_SUB, :].astype(BF16)
            z = _dot_nt(q, k) * scale + bias
            sp = _softplus(z)
            log_keep = -sp
            if diagonal:
                visible = (_iota((rows, SB_SUB), 1) + c * SB_SUB) < (_iota((rows, SB_SUB), 0) + r0)
                log_keep = jnp.where(visible, log_keep, 0.0)
            carry = carry_ref[r0:tq, :]
            after = _rev_excl_cumsum_lanes(log_keep, upper) + carry
            w = jnp.exp(z - sp + after)
            if diagonal:
                w = jnp.where(visible, w, 0.0)
            acc_ref[r0:tq, :] += _dot(w.astype(BF16), v)
            carry_ref[r0:tq, :] = carry + jnp.sum(log_keep, axis=1, keepdims=True)

    @pl.when(kj == qi)
    def _():
        sweep(True)

    @pl.when(kj != qi)
    def _():
        sweep(False)

    @pl.when(kj == 0)
    def _():
        o_ref[...] = acc_ref[...].astype(o_ref.dtype)


def _sb_prompt(proj3, bias_lanes):
    b, seq, _ = proj3.shape
    nq = seq // SB_TQ
    q_of, k_of = [], []
    for qi in range(nq):
        for kj in range(qi, -1, -1):
            q_of.append(qi)
            k_of.append(kj)
    q_tab = jnp.asarray(q_of, jnp.int32)
    k_tab = jnp.asarray(k_of, jnp.int32)
    kb = COL_SBK // HEAD_DIM
    vb = COL_SBV // HEAD_DIM
    grid_spec = pltpu.PrefetchScalarGridSpec(
        num_scalar_prefetch=2,
        grid=(b, SB_HEADS, len(q_of)),
        in_specs=[
            pl.BlockSpec((None, SB_TQ, HEAD_DIM), lambda bi, h, p, qt, kt: (bi, qt[p], h)),
            pl.BlockSpec((None, SB_TQ, HEAD_DIM), lambda bi, h, p, qt, kt: (bi, kt[p], kb + h)),
            pl.BlockSpec((None, SB_TQ, HEAD_DIM), lambda bi, h, p, qt, kt: (bi, kt[p], vb + h)),
            pl.BlockSpec((None, 1, LANES), lambda bi, h, p, qt, kt: (h, 0, 0)),
        ],
        out_specs=pl.BlockSpec((None, SB_TQ, HEAD_DIM), lambda bi, h, p, qt, kt: (bi, qt[p], h)),
        scratch_shapes=[pltpu.VMEM((SB_TQ, HEAD_DIM), F32), pltpu.VMEM((SB_TQ, LANES), F32)],
    )
    return pl.pallas_call(
        _sb_prompt_kernel,
        grid_spec=grid_spec,
        out_shape=jax.ShapeDtypeStruct((b, seq, SB_WIDTH), BF16),
        compiler_params=_cparams(("parallel", "parallel", "arbitrary")),
        name="sb_prompt",
    )(q_tab, k_tab, proj3, proj3, proj3, bias_lanes)


def _sb_decode_kernel(pt_ref, q_ref, bias_ref, *refs):
    k_refs = refs[:DEC_PAGES]
    v_refs = refs[DEC_PAGES:2 * DEC_PAGES]
    o_ref, acc_ref, carry_ref = refs[2 * DEC_PAGES:]
    g = pl.program_id(1)
    scale = HEAD_DIM ** -0.5
    rows = PAGE * SB_HEADS
    upper = (_iota((LANES, LANES), 0) > _iota((LANES, LANES), 1)).astype(BF16)
    own_head = jnp.bitwise_and(_iota((SB_HEADS, rows), 1), SB_HEADS - 1) == _iota((SB_HEADS, rows), 0)

    @pl.when(g == 0)
    def _():
        acc_ref[...] = jnp.zeros_like(acc_ref)
        carry_ref[...] = jnp.zeros_like(carry_ref)

    q = q_ref[...].astype(BF16)
    bias = bias_ref[...]
    for p in reversed(range(DEC_PAGES)):
        k = k_refs[p][...].astype(BF16)
        v = v_refs[p][...].astype(BF16)
        z = _dot_nt(q, k) * scale + bias
        sp = _softplus(z)
        log_keep = jnp.where(own_head, -sp, 0.0)
        n_chunks = rows // LANES
        stacked = jnp.concatenate([log_keep[:, c * LANES:(c + 1) * LANES] for c in range(n_chunks)], axis=0)
        hi = stacked.astype(BF16)
        lo = (stacked - hi.astype(F32)).astype(BF16)
        cum = _dot(jnp.concatenate([hi, lo], axis=0), upper)
        cum = cum[:n_chunks * SB_HEADS] + cum[n_chunks * SB_HEADS:]
        total = jnp.sum(stacked, axis=1, keepdims=True)
        run = carry_ref[...]
        after = [None] * n_chunks
        for c in reversed(range(n_chunks)):
            after[c] = cum[c * SB_HEADS:(c + 1) * SB_HEADS] + run
            run = run + total[c * SB_HEADS:(c + 1) * SB_HEADS]
        carry_ref[...] = run
        w = jnp.where(own_head, jnp.exp(z - sp + jnp.concatenate(after, axis=1)), 0.0)
        acc_ref[...] += _dot(w.astype(BF16), v)

    @pl.when(g == pl.num_programs(1) - 1)
    def _():
        o_ref[...] = acc_ref[...]


def _sb_decode(q_heads, bias_rows, cache_k3, cache_v3, page_table, layer, n_phys):
    nb = q_heads.shape[0]
    rows = PAGE * SB_HEADS
    n_pages = page_table.shape[1]
    n_groups = n_pages // DEC_PAGES
    base = layer * n_phys

    def page_map(p):
        def index_map(bi, g, pt):
            return (base + pt[bi, (n_groups - 1 - g) * DEC_PAGES + p], 0, 0)
        return index_map

    page_specs = [pl.BlockSpec((None, rows, HEAD_DIM), page_map(p)) for p in range(DEC_PAGES)]
    grid_spec = pltpu.PrefetchScalarGridSpec(
        num_scalar_prefetch=1,
        grid=(nb, n_groups),
        in_specs=[pl.BlockSpec((None, SB_HEADS, HEAD_DIM), lambda bi, g, pt: (bi, 0, 0)),
                  pl.BlockSpec((SB_HEADS, rows), lambda bi, g, pt: (0, 0))] + page_specs + page_specs,
        out_specs=pl.BlockSpec((None, SB_HEADS, HEAD_DIM), lambda bi, g, pt: (bi, 0, 0)),
        scratch_shapes=[pltpu.VMEM((SB_HEADS, HEAD_DIM), F32), pltpu.VMEM((SB_HEADS, LANES), F32)],
    )
    return pl.pallas_call(
        _sb_decode_kernel,
        grid_spec=grid_spec,
        out_shape=jax.ShapeDtypeStruct((nb, SB_HEADS, HEAD_DIM), F32),
        compiler_params=_cparams(("parallel", "arbitrary")),
        name="sb_decode",
    )(page_table, q_heads, bias_rows, *([cache_k3] * DEC_PAGES), *([cache_v3] * DEC_PAGES))


def _unit_lower_inverse(n_strict):
    eye = (_iota((CHUNK, CHUNK), 0) == _iota((CHUNK, CHUNK), 1)).astype(F32)
    m = -n_strict
    t = eye + m
    power = 2
    while power < CHUNK:
        m = _dot(m, m)
        t = t + _dot(t, m)
        power *= 2
    return t


def _gdn_prompt_kernel(x_ref, z_ref, misc_ref, convw_ref, par_ref, nw_ref, o_ref, s_ref,
                       xbuf, ybuf, gbuf, bbuf):
    l = pl.program_id(1)
    lb = x_ref.shape[0]
    halo = 8

    @pl.when(l == 0)
    def _():
        s_ref[...] = jnp.zeros_like(s_ref)
        xbuf[0:halo, :] = jnp.zeros((halo, CONV_CH), F32)

    @pl.when(l != 0)
    def _():
        xbuf[0:halo, :] = xbuf[lb:lb + halo, :]

    xbuf[halo:halo + lb, :] = x_ref[...]
    y = jnp.zeros((lb, CONV_CH), F32)
    for i in range(GDN_CONV):
        off = halo - (GDN_CONV - 1) + i
        y = y + xbuf[off:off + lb, :] * convw_ref[i:i + 1, :]
    ybuf[...] = _silu(y)

    misc = misc_ref[...]
    neg_rate = -jnp.exp(par_ref[0:1, :])
    gbuf[...] = neg_rate * _softplus(misc + par_ref[1:2, :])
    bbuf[...] = _sigmoid(misc)

    row = _iota((CHUNK, CHUNK), 0)
    col = _iota((CHUNK, CHUNK), 1)
    lower = row >= col
    strict = row > col
    eye = (row == col).astype(F32)
    lower_f = lower.astype(F32)
    nw = nw_ref[...]

    def chunk_body(c, carry):
        r = pl.multiple_of(c * CHUNK, CHUNK)
        gcum = _dot(lower_f, gbuf[pl.ds(r, CHUNK), :])
        beta_all = bbuf[pl.ds(r, CHUNK), :]
        for h in range(GDN_HEADS):
            q = ybuf[pl.ds(r, CHUNK), h * GDN_DK:(h + 1) * GDN_DK]
            k = ybuf[pl.ds(r, CHUNK), (GDN_HEADS + h) * GDN_DK:(GDN_HEADS + h + 1) * GDN_DK]
            v = ybuf[pl.ds(r, CHUNK), 2 * GDN_HEADS * GDN_DK + h * GDN_DV:
                     2 * GDN_HEADS * GDN_DK + (h + 1) * GDN_DV]
            q = q * lax.rsqrt(jnp.sum(q * q, axis=-1, keepdims=True) + RMS_EPS) * (GDN_DK ** -0.5)
            k = k * lax.rsqrt(jnp.sum(k * k, axis=-1, keepdims=True) + RMS_EPS)
            beta = beta_all[:, MISC_BETA + h:MISC_BETA + h + 1]
            gcol = gcum[:, MISC_A + h:MISC_A + h + 1]
            grow = jnp.sum(eye * gcol, axis=0, keepdims=True)
            decay = jnp.where(lower, jnp.exp(jnp.minimum(gcol - grow, 0.0)), 0.0)
            kb = k * beta
            k16 = k.astype(BF16)
            kk = _dot_nt(kb.astype(BF16), k16) * decay
            t_inv = _unit_lower_inverse(jnp.where(strict, kk, 0.0))
            e_g = jnp.exp(gcol)
            u = _dot(t_inv, v * beta)
            w = _dot(t_inv, kb * e_g)
            intra = _dot_nt(q.astype(BF16), k16) * decay
            s = s_ref[h]
            s16 = s.astype(BF16)
            v_new = u - _dot(w.astype(BF16), s16)
            o = _dot((q * e_g).astype(BF16), s16) + _dot(intra.astype(BF16), v_new.astype(BF16))
            g_last = gcol[CHUNK - 1:CHUNK, :]
            s_ref[h] = s * jnp.exp(g_last) + _dot_tn(k * jnp.exp(g_last - gcol), v_new)
            zg = z_ref[pl.ds(r, CHUNK), h * GDN_DV:(h + 1) * GDN_DV]
            o_ref[pl.ds(r, CHUNK), h * GDN_DV:(h + 1) * GDN_DV] = _gated_rms(o, zg, nw).astype(o_ref.dtype)
        return carry

    lax.fori_loop(0, lb // CHUNK, chunk_body, 0)


GDN_LB = 1024


def _gdn_prompt(proj3, conv_w, par, nw):
    b, seq, _ = proj3.shape
    lb = GDN_LB
    return pl.pallas_call(
        _gdn_prompt_kernel,
        grid=(b, seq // lb),
        in_specs=[
            pl.BlockSpec((None, lb, CONV_CH), lambda bi, l: (bi, l, COL_CONV // CONV_CH)),
            pl.BlockSpec((None, lb, GDN_WIDTH), lambda bi, l: (bi, l, COL_GZ // GDN_WIDTH)),
            pl.BlockSpec((None, lb, LANES), lambda bi, l: (bi, l, COL_MISC // LANES)),
            pl.BlockSpec((GDN_CONV, CONV_CH), lambda bi, l: (0, 0)),
            pl.BlockSpec((8, LANES), lambda bi, l: (0, 0)),
            pl.BlockSpec((1, GDN_DV), lambda bi, l: (0, 0)),
        ],
        out_specs=[
            pl.BlockSpec((None, lb, GDN_WIDTH), lambda bi, l: (bi, l, 0)),
            pl.BlockSpec((None, GDN_HEADS, GDN_DK, GDN_DV), lambda bi, l: (bi, 0, 0, 0)),
        ],
        out_shape=[jax.ShapeDtypeStruct((b, seq, GDN_WIDTH), BF16),
                   jax.ShapeDtypeStruct((b, GDN_HEADS, GDN_DK, GDN_DV), F32)],
        scratch_shapes=[pltpu.VMEM((lb + 8, CONV_CH), F32), pltpu.VMEM((lb, CONV_CH), F32),
                        pltpu.VMEM((lb, LANES), F32), pltpu.VMEM((lb, LANES), F32)],
        compiler_params=_cparams(("parallel", "arbitrary")),
        name="gdn_prompt",
    )(proj3, proj3, proj3, conv_w, par, nw)


def _gla_prompt_kernel(x_ref, gate_ref, misc_ref, wg_ref, bg_ref, nw_ref, o_ref, s_ref, abuf):
    l = pl.program_id(1)
    lb = x_ref.shape[0]
    kw = GLA_HEADS * GLA_DK

    @pl.when(l == 0)
    def _():
        s_ref[...] = jnp.zeros_like(s_ref)

    abuf[...] = _log_sigmoid(_dot(misc_ref[...], wg_ref[...]) + bg_ref[...]) * (1.0 / GLA_TAU)

    row = _iota((CHUNK, CHUNK), 0)
    col = _iota((CHUNK, CHUNK), 1)
    lower = row >= col
    lower_f = lower.astype(F32)
    eye_k = (_iota((GLA_DK, GLA_DK), 0) == _iota((GLA_DK, GLA_DK), 1)).astype(F32)
    key_row = _iota((CHUNK, GLA_DK), 0)
    nw = nw_ref[...]

    def chunk_body(c, carry):
        r = pl.multiple_of(c * CHUNK, CHUNK)
        bcum = _dot(lower_f, abuf[pl.ds(r, CHUNK), :])
        for h in range(GLA_HEADS):
            q = x_ref[pl.ds(r, CHUNK), h * GLA_DK:(h + 1) * GLA_DK] * (GLA_DK ** -0.5)
            k = x_ref[pl.ds(r, CHUNK), kw + h * GLA_DK:kw + (h + 1) * GLA_DK]
            v = x_ref[pl.ds(r, CHUNK), 2 * kw + h * GLA_DV:2 * kw + (h + 1) * GLA_DV]
            bh = bcum[:, h * GLA_DK:(h + 1) * GLA_DK]
            pieces = []
            for i in range(CHUNK // GLA_SUB):
                lo, hi = i * GLA_SUB, (i + 1) * GLA_SUB
                ref = bh[lo:lo + 1, :]
                qn = q[lo:hi, :] * jnp.exp(bh[lo:hi, :] - ref)
                kn = k * jnp.exp(jnp.where(key_row < hi, ref - bh, 0.0))
                pieces.append(_dot_nt(qn.astype(BF16), kn.astype(BF16)))
            att = jnp.where(lower, jnp.concatenate(pieces, axis=0), 0.0)
            s = s_ref[h]
            o = _dot((q * jnp.exp(bh)).astype(BF16), s.astype(BF16)) + _dot(att.astype(BF16), v.astype(BF16))
            b_last = bh[CHUNK - 1:CHUNK, :]
            s_ref[h] = _dot(eye_k * jnp.exp(b_last), s) + _dot_tn(k * jnp.exp(b_last - bh), v)
            zg = gate_ref[pl.ds(r, CHUNK), h * GLA_DV:(h + 1) * GLA_DV]
            o_ref[pl.ds(r, CHUNK), h * GLA_DV:(h + 1) * GLA_DV] = _gated_rms(o, zg, nw).astype(o_ref.dtype)
        return carry

    lax.fori_loop(0, lb // CHUNK, chunk_body, 0)


GLA_LB = 1024


def _gla_prompt(proj3, wg_pad, bg, nw):
    b, seq, _ = proj3.shape
    lb = GLA_LB
    xw = 2 * GLA_HEADS * GLA_DK + GLA_WIDTH
    return pl.pallas_call(
        _gla_prompt_kernel,
        grid=(b, seq // lb),
        in_specs=[
            pl.BlockSpec((None, lb, xw), lambda bi, l: (bi, l, COL_LQKV // xw)),
            pl.BlockSpec((None, lb, GLA_WIDTH), lambda bi, l: (bi, l, COL_LG // GLA_WIDTH)),
            pl.BlockSpec((None, lb, LANES), lambda bi, l: (bi, l, COL_MISC // LANES)),
            pl.BlockSpec((LANES, GLA_HEADS * GLA_DK), lambda bi, l: (0, 0)),
            pl.BlockSpec((1, GLA_HEADS * GLA_DK), lambda bi, l: (0, 0)),
            pl.BlockSpec((1, GLA_DV), lambda bi, l: (0, 0)),
        ],
        out_specs=[
            pl.BlockSpec((None, lb, GLA_WIDTH), lambda bi, l: (bi, l, 0)),
            pl.BlockSpec((None, GLA_HEADS, GLA_DK, GLA_DV), lambda bi, l: (bi, 0, 0, 0)),
        ],
        out_shape=[jax.ShapeDtypeStruct((b, seq, GLA_WIDTH), BF16),
                   jax.ShapeDtypeStruct((b, GLA_HEADS, GLA_DK, GLA_DV), F32)],
        scratch_shapes=[pltpu.VMEM((lb, GLA_HEADS * GLA_DK), F32)],
        compiler_params=_cparams(("parallel", "arbitrary")),
        name="gla_prompt",
    )(proj3, proj3, proj3, wg_pad, bg, nw)


def _to_column(row, n):
    eye = (_iota((n, n), 0) == _iota((n, n), 1)).astype(F32)
    return jnp.sum(eye * row, axis=1, keepdims=True)


def _recurrent_step_kernel(x_ref, sg_ref, sc_ref, sl_ref, convw_ref, par_ref, gnw_ref,
                           wg_ref, bg_ref, lnw_ref, og_ref, ol_ref, sg_out, sc_out, sl_out):
    xrow = x_ref[...]
    new = xrow[:, COL_CONV:COL_CONV + CONV_CH]
    old = sc_ref[...]
    sc_out[...] = jnp.concatenate([old[1:GDN_CONV - 1, :], new], axis=0)
    y = new * convw_ref[GDN_CONV - 1:GDN_CONV, :]
    for i in range(GDN_CONV - 1):
        y = y + old[i:i + 1, :] * convw_ref[i:i + 1, :]
    y = _silu(y)

    misc = xrow[:, COL_MISC:COL_MISC + LANES]
    g_all = -jnp.exp(par_ref[0:1, :]) * _softplus(misc + par_ref[1:2, :])
    beta_all = _sigmoid(misc)
    gnw = gnw_ref[...]
    for h in range(GDN_HEADS):
        q = y[:, h * GDN_DK:(h + 1) * GDN_DK]
        k = y[:, (GDN_HEADS + h) * GDN_DK:(GDN_HEADS + h + 1) * GDN_DK]
        v = y[:, 2 * GDN_HEADS * GDN_DK + h * GDN_DV:2 * GDN_HEADS * GDN_DK + (h + 1) * GDN_DV]
        q = q * lax.rsqrt(jnp.sum(q * q, axis=-1, keepdims=True) + RMS_EPS) * (GDN_DK ** -0.5)
        k = k * lax.rsqrt(jnp.sum(k * k, axis=-1, keepdims=True) + RMS_EPS)
        beta = beta_all[:, MISC_BETA + h:MISC_BETA + h + 1]
        e_g = jnp.exp(g_all[:, MISC_A + h:MISC_A + h + 1])
        s = sg_ref[h]
        kcol = _to_column(k, GDN_DK)
        qcol = _to_column(q, GDN_DK)
        v_new = beta * (v - e_g * jnp.sum(s * kcol, axis=0, keepdims=True))
        qk = jnp.sum(q * k, axis=-1, keepdims=True)
        o = e_g * jnp.sum(s * qcol, axis=0, keepdims=True) + qk * v_new
        sg_out[h] = s * e_g + kcol * v_new
        zg = xrow[:, COL_GZ + h * GDN_DV:COL_GZ + (h + 1) * GDN_DV]
        og_ref[:, h * GDN_DV:(h + 1) * GDN_DV] = _gated_rms(o, zg, gnw).astype(og_ref.dtype)

    kw = GLA_HEADS * GLA_DK
    log_a = _log_sigmoid(_dot(misc, wg_ref[...]) + bg_ref[...]) * (1.0 / GLA_TAU)
    lnw = lnw_ref[...]
    for h in range(GLA_HEADS):
        q = xrow[:, COL_LQKV + h * GLA_DK:COL_LQKV + (h + 1) * GLA_DK] * (GLA_DK ** -0.5)
        k = xrow[:, COL_LQKV + kw + h * GLA_DK:COL_LQKV + kw + (h + 1) * GLA_DK]
        v = xrow[:, COL_LQKV + 2 * kw + h * GLA_DV:COL_LQKV + 2 * kw + (h + 1) * GLA_DV]
        a = jnp.exp(log_a[:, h * GLA_DK:(h + 1) * GLA_DK])
        s = sl_ref[h]
        qk = jnp.sum(q * k, axis=-1, keepdims=True)
        o = jnp.sum(s * _to_column(q * a, GLA_DK), axis=0, keepdims=True) + qk * v
        sl_out[h] = _to_column(a, GLA_DK) * s + _to_column(k, GLA_DK) * v
        zg = xrow[:, COL_LG + h * GLA_DV:COL_LG + (h + 1) * GLA_DV]
        ol_ref[:, h * GLA_DV:(h + 1) * GLA_DV] = _gated_rms(o, zg, lnw).astype(ol_ref.dtype)


def _recurrent_step(proj_s3, state_gdn, state_conv, state_gla, layer, conv_w, par, gnw, wg_pad, bg, lnw):
    nb = proj_s3.shape[0]
    const2 = lambda bi: (0, 0)
    return pl.pallas_call(
        _recurrent_step_kernel,
        grid=(nb,),
        in_specs=[
            pl.BlockSpec((None, 1, N_PROJ), lambda bi: (bi, 0, 0)),
            pl.BlockSpec((None, None, GDN_HEADS, GDN_DK, GDN_DV), lambda bi: (layer, bi, 0, 0, 0)),
            pl.BlockSpec((None, None, GDN_CONV - 1, CONV_CH), lambda bi: (layer, bi, 0, 0)),
            pl.BlockSpec((None, None, GLA_HEADS, GLA_DK, GLA_DV), lambda bi: (layer, bi, 0, 0, 0)),
            pl.BlockSpec((GDN_CONV, CONV_CH), const2),
            pl.BlockSpec((8, LANES), const2),
            pl.BlockSpec((1, GDN_DV), const2),
            pl.BlockSpec((LANES, GLA_HEADS * GLA_DK), const2),
            pl.BlockSpec((1, GLA_HEADS * GLA_DK), const2),
            pl.BlockSpec((1, GLA_DV), const2),
        ],
        out_specs=[
            pl.BlockSpec((None, 1, GDN_WIDTH), lambda bi: (bi, 0, 0)),
            pl.BlockSpec((None, 1, GLA_WIDTH), lambda bi: (bi, 0, 0)),
            pl.BlockSpec((None, GDN_HEADS, GDN_DK, GDN_DV), lambda bi: (bi, 0, 0, 0)),
            pl.BlockSpec((None, GDN_CONV - 1, CONV_CH), lambda bi: (bi, 0, 0)),
            pl.BlockSpec((None, GLA_HEADS, GLA_DK, GLA_DV), lambda bi: (bi, 0, 0, 0)),
        ],
        out_shape=[
            jax.ShapeDtypeStruct((nb, 1, GDN_WIDTH), F32),
            jax.ShapeDtypeStruct((nb, 1, GLA_WIDTH), F32),
            jax.ShapeDtypeStruct((nb, GDN_HEADS, GDN_DK, GDN_DV), F32),
            jax.ShapeDtypeStruct((nb, GDN_CONV - 1, CONV_CH), F32),
            jax.ShapeDtypeStruct((nb, GLA_HEADS, GLA_DK, GLA_DV), F32),
        ],
        compiler_params=_cparams(("parallel",)),
        name="recurrent_step",
    )(proj_s3, state_gdn, state_conv, state_gla, conv_w, par, gnw, wg_pad, bg, lnw)


def _mix_out_kernel(x_ref, osb_ref, ogdn_ref, ogla_ref, w_ref, g_ref, b_ref, rw_ref, rb_ref,
                    x1_ref, x1h_ref, logit_ref, *, alpha):
    mixed = (_dot(osb_ref[...].astype(BF16), w_ref[0:SB_WIDTH, :])
             + _dot(ogdn_ref[...].astype(BF16), w_ref[SB_WIDTH:SB_WIDTH + GDN_WIDTH, :])
             + _dot(ogla_ref[...].astype(BF16), w_ref[SB_WIDTH + GDN_WIDTH:, :]))
    x1 = _layer_norm(alpha * x_ref[...] + mixed, g_ref[...], b_ref[...])
    x1_ref[...] = x1
    x1h_ref[...] = x1.astype(BF16)
    logit_ref[...] = _dot(x1, rw_ref[...]) + rb_ref[...]


def _mix_out(x, o_sb, o_gdn, o_gla, w_out, ln_g, ln_b, router_w, router_b, alpha, tm):
    m, d = x.shape
    row = lambda i: (i, 0)
    const = lambda i: (0, 0)
    return pl.pallas_call(
        functools.partial(_mix_out_kernel, alpha=alpha),
        grid=(m // tm,),
        in_specs=[
            pl.BlockSpec((tm, d), row),
            pl.BlockSpec((tm, SB_WIDTH), row),
            pl.BlockSpec((tm, GDN_WIDTH), row),
            pl.BlockSpec((tm, GLA_WIDTH), row),
            pl.BlockSpec((d, d), const),
            pl.BlockSpec((1, d), const),
            pl.BlockSpec((1, d), const),
            pl.BlockSpec((d, LANES), const),
            pl.BlockSpec((1, LANES), const),
        ],
        out_specs=[pl.BlockSpec((tm, d), row), pl.BlockSpec((tm, d), row), pl.BlockSpec((tm, LANES), row)],
        out_shape=[jax.ShapeDtypeStruct((m, d), F32), jax.ShapeDtypeStruct((m, d), BF16),
                   jax.ShapeDtypeStruct((m, LANES), F32)],
        compiler_params=_cparams(("parallel",)),
        name="mix_out_ln_router",
    )(x, o_sb, o_gdn, o_gla, w_out, ln_g, ln_b, router_w, router_b)


def _expert_changed(be_ref, i):
    prev = be_ref[jnp.maximum(i - 1, 0)]
    return jnp.logical_or(i == 0, be_ref[i] != prev)


def _moe_up_kernel(be_ref, x_ref, wg_ref, wu_ref, bg_ref, bu_ref, h_ref, wg_bf, wu_bf):
    i = pl.program_id(1)

    @pl.when(_expert_changed(be_ref, i))
    def _():
        wg_bf[...] = wg_ref[...].astype(BF16)
        wu_bf[...] = wu_ref[...].astype(BF16)

    x = x_ref[...]
    gate = jnp.minimum(_dot(x, wg_bf[...]) + bg_ref[...], SWIGLU_LIMIT)
    up = jnp.clip(_dot(x, wu_bf[...]) + bu_ref[...], -SWIGLU_LIMIT, SWIGLU_LIMIT)
    h_ref[...] = ((up + 1.0) * (gate * _sigmoid(SWIGLU_ALPHA * gate))).astype(h_ref.dtype)


def _moe_down_kernel(be_ref, h_ref, w_ref, b_ref, gate_ref, y_ref, w_bf):
    i = pl.program_id(1)

    @pl.when(_expert_changed(be_ref, i))
    def _():
        w_bf[...] = w_ref[...].astype(BF16)

    y_ref[...] = (_dot(h_ref[...], w_bf[...]) + b_ref[...]) * gate_ref[...]


def _moe_experts(x_rows, row_gate, blk_expert, w_gu, b_gu4, w_dn, b_dn4, layer):
    n_rows, d = x_rows.shape
    n_blocks = n_rows // MOE_BLK
    nj = MOE_FF // MOE_TN
    up_spec = pltpu.PrefetchScalarGridSpec(
        num_scalar_prefetch=1,
        grid=(nj, n_blocks),
        in_specs=[
            pl.BlockSpec((MOE_BLK, d), lambda j, i, be: (i, 0)),
            pl.BlockSpec((None, None, d, MOE_TN), lambda j, i, be: (layer, be[i], 0, j)),
            pl.BlockSpec((None, None, d, MOE_TN), lambda j, i, be: (layer, be[i], 0, nj + j)),
            pl.BlockSpec((None, None, 1, MOE_TN), lambda j, i, be: (layer, be[i], 0, j)),
            pl.BlockSpec((None, None, 1, MOE_TN), lambda j, i, be: (layer, be[i], 0, nj + j)),
        ],
        out_specs=pl.BlockSpec((MOE_BLK, MOE_TN), lambda j, i, be: (i, j)),
        scratch_shapes=[pltpu.VMEM((d, MOE_TN), BF16), pltpu.VMEM((d, MOE_TN), BF16)],
    )
    hidden = pl.pallas_call(
        _moe_up_kernel,
        grid_spec=up_spec,
        out_shape=jax.ShapeDtypeStruct((n_rows, MOE_FF), BF16),
        compiler_params=_cparams(("arbitrary", "arbitrary")),
        name="moe_up",
    )(blk_expert, x_rows, w_gu, w_gu, b_gu4, b_gu4)

    nj_dn = d // MOE_TN
    down_spec = pltpu.PrefetchScalarGridSpec(
        num_scalar_prefetch=1,
        grid=(nj_dn, n_blocks),
        in_specs=[
            pl.BlockSpec((MOE_BLK, MOE_FF), lambda j, i, be: (i, 0)),
            pl.BlockSpec((None, None, MOE_FF, MOE_TN), lambda j, i, be: (layer, be[i], 0, j)),
            pl.BlockSpec((None, None, 1, MOE_TN), lambda j, i, be: (layer, be[i], 0, j)),
            pl.BlockSpec((MOE_BLK, 1), lambda j, i, be: (i, 0)),
        ],
        out_specs=pl.BlockSpec((MOE_BLK, MOE_TN), lambda j, i, be: (i, j)),
        scratch_shapes=[pltpu.VMEM((MOE_FF, MOE_TN), BF16)],
    )
    return pl.pallas_call(
        _moe_down_kernel,
        grid_spec=down_spec,
        out_shape=jax.ShapeDtypeStruct((n_rows, d), F32),
        compiler_params=_cparams(("arbitrary", "arbitrary")),
        name="moe_down",
    )(blk_expert, hidden, w_dn, b_dn4, row_gate)


def _route(logits):
    t = logits.shape[0]
    top_val, top_idx = lax.top_k(logits, TOP_K)
    gates = jax.nn.softmax(top_val, axis=-1)
    n_assign = t * TOP_K
    flat_e = top_idx.reshape(n_assign).astype(jnp.int32)
    flat_gate = gates.reshape(n_assign)
    order = jnp.argsort(flat_e).astype(jnp.int32)
    rank = jnp.argsort(order).astype(jnp.int32)
    counts = jnp.bincount(flat_e, length=N_EXPERTS).astype(jnp.int32)
    padded = (counts + MOE_BLK - 1) // MOE_BLK * MOE_BLK
    pad_end = jnp.cumsum(padded)
    pad_start = pad_end - padded
    start = jnp.cumsum(counts) - counts
    n_blocks = -(-(n_assign + N_EXPERTS * (MOE_BLK - 1)) // MOE_BLK)
    blk_first = jnp.arange(n_blocks, dtype=jnp.int32) * MOE_BLK
    blk_expert = jnp.minimum(
        jnp.sum((pad_end[None, :] <= blk_first[:, None]).astype(jnp.int32), axis=1), N_EXPERTS - 1)
    pos = (jnp.arange(n_blocks, dtype=jnp.int32) * MOE_BLK - pad_start[blk_expert])[:, None] \
        + jnp.arange(MOE_BLK, dtype=jnp.int32)[None, :]
    live = pos < counts[blk_expert][:, None]
    src = jnp.where(live, start[blk_expert][:, None] + pos, 0).reshape(-1)
    assign = order[src]
    live = live.reshape(-1)
    row_tok = jnp.where(live, assign // TOP_K, 0)
    row_gate = jnp.where(live, flat_gate[assign], 0.0)
    dest = ((pad_start - start)[flat_e] + rank).reshape(t, TOP_K)
    return row_tok, row_gate, blk_expert, dest


def _ffn_out_kernel(x1_ref, moe_ref, p_ref, wp_ref, wg_ref, g_ref, b_ref, x3_ref, *, alpha):
    g = g_ref[...]
    b = b_ref[...]
    x2 = _layer_norm(alpha * x1_ref[...] + moe_ref[...], g[0:1, :], b[0:1, :])
    ple = _dot(p_ref[...].astype(BF16), wp_ref[...]) * _sigmoid(_dot(x2.astype(BF16), wg_ref[...]))
    x3_ref[...] = _layer_norm(alpha * x2 + ple, g[1:2, :], b[1:2, :])


def _ffn_out(x1, moe, p, wp, wg, ln_g2, ln_b2, alpha, tm):
    m, d = x1.shape
    pd = p.shape[1]
    row = lambda i: (i, 0)
    const = lambda i: (0, 0)
    return pl.pallas_call(
        functools.partial(_ffn_out_kernel, alpha=alpha),
        grid=(m // tm,),
        in_specs=[
            pl.BlockSpec((tm, d), row),
            pl.BlockSpec((tm, d), row),
            pl.BlockSpec((tm, pd), row),
            pl.BlockSpec((pd, d), const),
            pl.BlockSpec((d, d), const),
            pl.BlockSpec((2, d), const),
            pl.BlockSpec((2, d), const),
        ],
        out_specs=pl.BlockSpec((tm, d), row),
        out_shape=jax.ShapeDtypeStruct((m, d), F32),
        compiler_params=_cparams(("parallel",)),
        name="ffn_out_ple_ln",
    )(x1, moe, p, wp, wg, ln_g2, ln_b2)


def _permute_w_in(w):
    c = 5 * SB_WIDTH
    g_b = w[:, c:c + GDN_HEADS]
    g_a = w[:, c + GDN_HEADS:c + 2 * GDN_HEADS]
    c += 2 * GDN_HEADS
    kw = GLA_HEADS * GLA_DK
    l_q = w[:, c:c + kw]
    l_k = w[:, c + kw:c + 2 * kw]
    l_v = w[:, c + 2 * kw:c + 2 * kw + GLA_WIDTH]
    c += 2 * kw + GLA_WIDTH
    l_r = w[:, c:c + GLA_RANK]
    l_g = w[:, c + GLA_RANK:c + GLA_RANK + GLA_WIDTH]
    used = COL_MISC + 2 * GDN_HEADS + GLA_RANK
    pad = jnp.zeros((w.shape[0], N_PROJ - used), w.dtype)
    return jnp.concatenate([w[:, :5 * SB_WIDTH], l_q, l_k, l_v, l_g, g_b, g_a, l_r, pad], axis=1).astype(BF16)


def kernel(x_prompt, x_sample, cache_k, cache_v, state_gdn, state_conv, state_gla, page_table, p_prompt, p_sample, w_in, sb_logit_bias, gdn_conv_w, gdn_a_log, gdn_dt_bias, gdn_norm_w, gla_w_gate, gla_b_gate, gla_norm_w, w_out, ln_g, ln_b, router_w, router_b, moe_w_gu, moe_b_gu, moe_w_dn, moe_b_dn, ple_w_proj, ple_w_gate):
    depth = w_in.shape[0]
    b, seq, d = x_prompt.shape
    nb = x_sample.shape[0]
    t_p = b * seq
    n_phys = cache_k.shape[1]
    alpha = (2 * depth) ** 0.25

    cache_k3 = cache_k.reshape(depth * n_phys, PAGE * SB_HEADS, HEAD_DIM)
    cache_v3 = cache_v.reshape(depth * n_phys, PAGE * SB_HEADS, HEAD_DIM)
    b_gu4 = moe_b_gu.reshape(depth, N_EXPERTS, 1, 2 * MOE_FF)
    b_dn4 = moe_b_dn.reshape(depth, N_EXPERTS, 1, d)

    h_p = x_prompt.reshape(t_p, d)
    h_s = x_sample.reshape(nb, d)
    outs_p = ([], [], [], [], [])
    outs_s = ([], [], [], [], [])
    for i in range(depth):
        w_in_i = _permute_w_in(w_in[i])
        w_out_i = w_out[i].astype(BF16)
        wp_i = ple_w_proj[i].astype(BF16)
        wgate_i = ple_w_gate[i].astype(BF16)
        bias_lanes = jnp.broadcast_to(sb_logit_bias[i][:, None, None], (SB_HEADS, 1, LANES))
        bias_rows = jnp.broadcast_to(sb_logit_bias[i][:, None], (SB_HEADS, PAGE * SB_HEADS))
        par = jnp.zeros((8, LANES), F32)
        par = par.at[0, MISC_A:MISC_A + GDN_HEADS].set(gdn_a_log[i])
        par = par.at[1, MISC_A:MISC_A + GDN_HEADS].set(gdn_dt_bias[i])
        gnw = gdn_norm_w[i].reshape(1, GDN_DV)
        lnw = gla_norm_w[i].reshape(1, GLA_DV)
        wg_pad = jnp.zeros((LANES, GLA_HEADS * GLA_DK), F32).at[MISC_R:MISC_R + GLA_RANK].set(gla_w_gate[i])
        bg = gla_b_gate[i].reshape(1, GLA_HEADS * GLA_DK)
        rw = jnp.zeros((d, LANES), F32).at[:, :N_EXPERTS].set(router_w[i])
        rb = jnp.zeros((1, LANES), F32).at[0, :N_EXPERTS].set(router_b[i])
        g0, b0 = ln_g[i, 0:1], ln_b[i, 0:1]

        proj_p = _project(h_p, w_in_i, 1024)
        proj_p3 = proj_p.reshape(b, seq, N_PROJ)
        o_sb = _sb_prompt(proj_p3, bias_lanes)
        o_gdn, gdn_state_p = _gdn_prompt(proj_p3, gdn_conv_w[i], par, gnw)
        o_gla, gla_state_p = _gla_prompt(proj_p3, wg_pad, bg, lnw)
        x1_p, x1h_p, logit_p = _mix_out(h_p, o_sb.reshape(t_p, SB_WIDTH), o_gdn.reshape(t_p, GDN_WIDTH),
                                        o_gla.reshape(t_p, GLA_WIDTH), w_out_i, g0, b0, rw, rb, alpha, 256)
        outs_p[0].append(proj_p3[:, :, COL_SBK:COL_SBK + SB_WIDTH].reshape(b, seq, SB_HEADS, HEAD_DIM))
        outs_p[1].append(proj_p3[:, :, COL_SBV:COL_SBV + SB_WIDTH].reshape(b, seq, SB_HEADS, HEAD_DIM))
        outs_p[2].append(gdn_state_p)
        outs_p[3].append(proj_p3[:, seq - (GDN_CONV - 1):, COL_CONV:COL_CONV + CONV_CH])
        outs_p[4].append(gla_state_p)

        proj_s = _project(h_s, w_in_i, nb)
        proj_s3 = proj_s.reshape(nb, 1, N_PROJ)
        q_heads_s = proj_s[:, COL_SBQ:COL_SBQ + SB_WIDTH].reshape(nb, SB_HEADS, HEAD_DIM)
        o_sb_s = _sb_decode(q_heads_s, bias_rows, cache_k3, cache_v3, page_table, i, n_phys)
        o_gdn_s, o_gla_s, gdn_state_s, conv_state_s, gla_state_s = _recurrent_step(
            proj_s3, state_gdn, state_conv, state_gla, i, gdn_conv_w[i], par, gnw, wg_pad, bg, lnw)
        x1_s, x1h_s, logit_s = _mix_out(h_s, o_sb_s.reshape(nb, SB_WIDTH), o_gdn_s.reshape(nb, GDN_WIDTH),
                                        o_gla_s.reshape(nb, GLA_WIDTH), w_out_i, g0, b0, rw, rb, alpha, nb)
        outs_s[0].append(proj_s[:, COL_SBK:COL_SBK + SB_WIDTH].reshape(nb, 1, SB_HEADS, HEAD_DIM))
        outs_s[1].append(proj_s[:, COL_SBV:COL_SBV + SB_WIDTH].reshape(nb, 1, SB_HEADS, HEAD_DIM))
        outs_s[2].append(gdn_state_s)
        outs_s[3].append(conv_state_s)
        outs_s[4].append(gla_state_s)

        logits = jnp.concatenate([logit_p[:, :N_EXPERTS], logit_s[:, :N_EXPERTS]], axis=0)
        row_tok, row_gate, blk_e, dest = _route(logits)
        x_rows = jnp.concatenate([x1h_p, x1h_s], axis=0)[row_tok]
        y_rows = _moe_experts(x_rows, row_gate[:, None], blk_e, moe_w_gu, b_gu4, moe_w_dn, b_dn4, i)
        moe_all = jnp.sum(y_rows[dest], axis=1)
        moe_p, moe_s = moe_all[:t_p], moe_all[t_p:]

        g12, b12 = ln_g[i, 1:3], ln_b[i, 1:3]
        h_p = _ffn_out(x1_p, moe_p, p_prompt[i].reshape(t_p, -1), wp_i, wgate_i, g12, b12, alpha, 256)
        h_s = _ffn_out(x1_s, moe_s, p_sample[i].reshape(nb, -1), wp_i, wgate_i, g12, b12, alpha, nb)

    y_prompt = h_p.reshape(b, seq, d)
    y_sample = h_s.reshape(nb, 1, d)
    return (y_prompt, y_sample,
            jnp.stack(outs_p[0]), jnp.stack(outs_p[1]), jnp.stack(outs_p[2]), jnp.stack(outs_p[3]),
            jnp.stack(outs_p[4]),
            jnp.stack(outs_s[0]), jnp.stack(outs_s[1]), jnp.stack(outs_s[2]), jnp.stack(outs_s[3]),
            jnp.stack(outs_s[4]))
```
